```python
import jax, jax.numpy as jnp
from jax import lax
import numpy as np

D_MODEL = 1024
BATCH = 8
SEQ = 8192
DEPTH = 2
DEC_BATCH = 32
DEC_SEQ = 32
PAST_LEN = 1024

CHUNK = 64
BAND_CHUNKS = 8
BAND = BAND_CHUNKS * CHUNK
N_A = DEPTH // 2
N_B = DEPTH - N_A
HEAD_DIM = 64
H_A = D_MODEL // HEAD_DIM
H_B = D_MODEL // HEAD_DIM
REL_CLIP = 128
Q_BLOCK = 128
PEER_HEADS = 8
N_KEYS = 128
N_EXPERTS = N_KEYS * N_KEYS
PEER_TOPK = 16
D_KEY = 256
D_KEY_HALF = D_KEY // 2
PEER_BLOCK = 256
EPS = 1e-6
NEG_INF = -1e30

kernel_name = "yoco_chunkband_fox_peer_stream_step"


def rms_norm(x, g):
    x32 = x.astype(jnp.float32)
    y = x32 * lax.rsqrt(jnp.mean(x32 * x32, axis=-1, keepdims=True) + EPS)
    return y.astype(x.dtype) * g


def mod_norm(x, g, shift, scale):
    return rms_norm(x, g) * (1.0 + scale) + shift


def band_attend(q, k, v, q_pos, k_pos, rel_bias):
    s = jnp.einsum('bqhd,bkhd->bhqk', q, k).astype(jnp.float32) * (HEAD_DIM ** -0.5)
    dist = q_pos[:, None] - k_pos[None, :]
    rel = jnp.clip(dist, -REL_CLIP, REL_CLIP) + REL_CLIP
    s = s + rel_bias[:, rel].astype(jnp.float32)
    dc = (q_pos // CHUNK)[:, None] - (k_pos // CHUNK)[None, :]
    valid = (k_pos[None, :] >= 0) & (dc >= 0) & (dc <= BAND_CHUNKS)
    s = jnp.where(valid, s, NEG_INF)
    p = jax.nn.softmax(s, axis=-1).astype(v.dtype)
    return jnp.einsum('bhqk,bkhd->bqhd', p, v)


def band_attend_prompt(q, k, v, rel_bias):
    b, s, h, d = q.shape
    nc = s // CHUNK
    k_pad = jnp.pad(k, ((0, 0), (BAND, 0), (0, 0), (0, 0)))
    v_pad = jnp.pad(v, ((0, 0), (BAND, 0), (0, 0), (0, 0)))
    q_ch = q.reshape(b, nc, CHUNK, h, d).swapaxes(0, 1)

    def one_chunk(args):
        qc, ci = args
        start = ci * CHUNK
        kb = lax.dynamic_slice_in_dim(k_pad, start, BAND + CHUNK, axis=1)
        vb = lax.dynamic_slice_in_dim(v_pad, start, BAND + CHUNK, axis=1)
        q_pos = start + jnp.arange(CHUNK)
        k_pos = start - BAND + jnp.arange(BAND + CHUNK)
        return band_attend(qc, kb, vb, q_pos, k_pos, rel_bias)

    o = lax.map(one_chunk, (q_ch, jnp.arange(nc)))
    return o.swapaxes(0, 1).reshape(b, s, h, d)


def forget_attend(q, k, v, cq, ck, q_pos, k_pos):
    s = jnp.einsum('bqhd,bkhd->bhqk', q, k).astype(jnp.float32) * (HEAD_DIM ** -0.5)
    s = s + jnp.swapaxes(cq, 1, 2)[:, :, :, None] - jnp.swapaxes(ck, 1, 2)[:, :, None, :]
    s = jnp.where(k_pos[None, :] <= q_pos[:, None], s, NEG_INF)
    p = jax.nn.softmax(s, axis=-1).astype(v.dtype)
    return jnp.einsum('bhqk,bkhd->bqhd', p, v)


def forget_attend_prompt(q, k, v, cum):
    b, s, h, d = q.shape
    nb = s // Q_BLOCK
    q_blk = q.reshape(b, nb, Q_BLOCK, h, d).swapaxes(0, 1)
    c_blk = cum.reshape(b, nb, Q_BLOCK, h).swapaxes(0, 1)
    k_pos = jnp.arange(s)

    def one_block(args):
        qb, cb, bi = args
        q_pos = bi * Q_BLOCK + jnp.arange(Q_BLOCK)
        return forget_attend(qb, k, v, cb, cum, q_pos, k_pos)

    o = lax.map(one_block, (q_blk, c_blk, jnp.arange(nb)))
    return o.swapaxes(0, 1).reshape(b, s, h, d)


def peer(h, w_pq, sub_keys, u, v):
    b, s, d = h.shape
    t = h.reshape(b * s, d)
    n_tok = b * s
    nb = -(-n_tok // PEER_BLOCK)
    t = jnp.pad(t, ((0, nb * PEER_BLOCK - n_tok), (0, 0))).reshape(nb, PEER_BLOCK, d)

    def one_block(tb):
        nt = tb.shape[0]
        q = (tb @ w_pq).reshape(nt, PEER_HEADS, 2, D_KEY_HALF)
        sc = jnp.einsum('thpd,phnd->thpn', q, sub_keys)
        top_s, top_i = lax.top_k(sc, PEER_TOPK)
        cand_s = top_s[:, :, 0, :, None] + top_s[:, :, 1, None, :]
        cand_i = top_i[:, :, 0, :, None] * N_KEYS + top_i[:, :, 1, None, :]
        cand_s = cand_s.reshape(nt, PEER_HEADS, PEER_TOPK * PEER_TOPK)
        cand_i = cand_i.reshape(nt, PEER_HEADS, PEER_TOPK * PEER_TOPK)
        fin_s, fin_pos = lax.top_k(cand_s, PEER_TOPK)
        idx = jnp.take_along_axis(cand_i, fin_pos, axis=-1)
        gate = jax.nn.softmax(fin_s.astype(jnp.float32), axis=-1).astype(tb.dtype)
        u_sel = jnp.take(u, idx, axis=0)
        act = jax.nn.gelu(jnp.einsum('td,thkd->thk', tb, u_sel))
        v_sel = jnp.take(v, idx, axis=0)
        return jnp.einsum('thk,thkd->td', gate * act, v_sel)

    out = lax.map(one_block, t).reshape(nb * PEER_BLOCK, d)[:n_tok]
    return out.reshape(b, s, d)


def trunk(x, c, cache_a_k, cache_a_v, cache_b_k, cache_b_v, cache_b_logf,
          w_ada, b_ada, g_norm, w_qkv_a, rel_bias_a, w_o_a, w_ada_kv, b_ada_kv, g_kv,
          w_kv, w_f, b_f, w_q_b, w_o_b, w_pq, sub_keys, peer_u, peer_v, g_final):
    prompt = cache_a_k is None
    b, s, _ = x.shape
    past = 0 if prompt else cache_b_k.shape[1]
    new_ak, new_av = [], []
    bk = bv = blogf = None
    kb_all = vb_all = cum = None
    for l in range(DEPTH):
        sh_m, sc_m, gt_m, sh_c, sc_c, gt_c = jnp.split((c @ w_ada[l] + b_ada[l])[:, None, :], 6, axis=-1)
        h = mod_norm(x, g_norm[l, 0], sh_m, sc_m)
        if l < N_A:
            qkv = (h @ w_qkv_a[l]).reshape(b, s, 3, H_A, HEAD_DIM)
            q, k, v = qkv[:, :, 0], qkv[:, :, 1], qkv[:, :, 2]
            if prompt:
                o = band_attend_prompt(q, k, v, rel_bias_a[l])
                keep = min(BAND, s)
                new_ak.append(k[:, s - keep:])
                new_av.append(v[:, s - keep:])
            else:
                win = cache_a_k.shape[2]
                ka_all = jnp.concatenate([cache_a_k[l], k], axis=1)
                va_all = jnp.concatenate([cache_a_v[l], v], axis=1)
                q_pos = past + jnp.arange(s)
                k_pos = past - win + jnp.arange(win + s)
                o = band_attend(q, ka_all, va_all, q_pos, k_pos, rel_bias_a[l])
                new_ak.append(k)
                new_av.append(v)
            o = o.reshape(b, s, H_A * HEAD_DIM) @ w_o_a[l]
        else:
            if l == N_A:
                sh_kv, sc_kv = jnp.split((c @ w_ada_kv + b_ada_kv)[:, None, :], 2, axis=-1)
                hkv = mod_norm(x, g_kv, sh_kv, sc_kv)
                kv = (hkv @ w_kv).reshape(b, s, 2, H_B, HEAD_DIM)
                bk, bv = kv[:, :, 0], kv[:, :, 1]
                blogf = jax.nn.log_sigmoid((hkv @ w_f + b_f).astype(jnp.float32))
                if prompt:
                    kb_all, vb_all, lf_all = bk, bv, blogf
                else:
                    kb_all = jnp.concatenate([cache_b_k, bk], axis=1)
                    vb_all = jnp.concatenate([cache_b_v, bv], axis=1)
                    lf_all = jnp.concatenate([cache_b_logf.astype(jnp.float32), blogf], axis=1)
                cum = jnp.cumsum(lf_all, axis=1)
            q = (h @ w_q_b[l - N_A]).reshape(b, s, H_B, HEAD_DIM)
            if prompt:
                o = forget_attend_prompt(q, kb_all, vb_all, cum)
            else:
                q_pos = past + jnp.arange(s)
                k_pos = jnp.arange(past + s)
                o = forget_attend(q, kb_all, vb_all, cum[:, past:], cum, q_pos, k_pos)
            o = o.reshape(b, s, H_B * HEAD_DIM) @ w_o_b[l - N_A]
        x = x + gt_m * o
        h = mod_norm(x, g_norm[l, 1], sh_c, sc_c)
        x = x + gt_c * peer(h, w_pq[l], sub_keys[l], peer_u[l], peer_v[l])
    y = rms_norm(x, g_final)
    return y, jnp.stack(new_ak), jnp.stack(new_av), bk, bv, blogf


def setup_inputs(seed: int = 0) -> dict:
    key = jax.random.key(seed)
    ks = jax.random.split(key, 32)
    D = D_MODEL
    a_win = min(BAND, PAST_LEN)

    def nrm(k, shape, scale):
        return jax.random.normal(k, shape, jnp.float32) * scale

    return {
        'x_prompt': nrm(ks[0], (BATCH, SEQ, D), 1.0),
        'x_sample': nrm(ks[1], (DEC_BATCH, DEC_SEQ, D), 1.0),
        'c_prompt': nrm(ks[2], (BATCH, D), 1.0),
        'c_sample': nrm(ks[3], (DEC_BATCH, D), 1.0),
        'cache_a_k': nrm(ks[4], (N_A, DEC_BATCH, a_win, H_A, HEAD_DIM), 1.0),
        'cache_a_v': nrm(ks[5], (N_A, DEC_BATCH, a_win, H_A, HEAD_DIM), 1.0),
        'cache_b_k': nrm(ks[6], (DEC_BATCH, PAST_LEN, H_B, HEAD_DIM), 1.0),
        'cache_b_v': nrm(ks[7], (DEC_BATCH, PAST_LEN, H_B, HEAD_DIM), 1.0),
        'cache_b_logf': jax.nn.log_sigmoid(3.0 + nrm(ks[8], (DEC_BATCH, PAST_LEN, H_B), 1.0)),
        'w_ada': nrm(ks[9], (DEPTH, D, 6 * D), 0.3 * D ** -0.5),
        'b_ada': nrm(ks[10], (DEPTH, 6 * D), 0.02),
        'g_norm': 1.0 + nrm(ks[11], (DEPTH, 2, D), 0.02),
        'w_qkv_a': nrm(ks[12], (N_A, D, 3 * H_A * HEAD_DIM), D ** -0.5),
        'rel_bias_a': nrm(ks[13], (N_A, H_A, 2 * REL_CLIP + 1), 0.3),
        'w_o_a': nrm(ks[14], (N_A, H_A * HEAD_DIM, D), (H_A * HEAD_DIM) ** -0.5),
        'w_ada_kv': nrm(ks[15], (D, 2 * D), 0.3 * D ** -0.5),
        'b_ada_kv': nrm(ks[16], (2 * D,), 0.02),
        'g_kv': 1.0 + nrm(ks[17], (D,), 0.02),
        'w_kv': nrm(ks[18], (D, 2 * H_B * HEAD_DIM), D ** -0.5),
        'w_f': nrm(ks[19], (D, H_B), 0.5 * D ** -0.5),
        'b_f': 3.0 + nrm(ks[20], (H_B,), 1.0),
        'w_q_b': nrm(ks[21], (N_B, D, H_B * HEAD_DIM), D ** -0.5),
        'w_o_b': nrm(ks[22], (N_B, H_B * HEAD_DIM, D), (H_B * HEAD_DIM) ** -0.5),
        'w_pq': nrm(ks[23], (DEPTH, D, PEER_HEADS * D_KEY), D ** -0.5),
        'sub_keys': nrm(ks[24], (DEPTH, 2, PEER_HEADS, N_KEYS, D_KEY_HALF), D_KEY_HALF ** -0.5),
        'peer_u': nrm(ks[25], (DEPTH, N_EXPERTS, D), D ** -0.5),
        'peer_v': nrm(ks[26], (DEPTH, N_EXPERTS, D), PEER_HEADS ** -0.5),
        'g_final': 1.0 + nrm(ks[27], (D,), 0.02),
    }


def reference(x_prompt, x_sample, c_prompt, c_sample, cache_a_k, cache_a_v, cache_b_k, cache_b_v,
              cache_b_logf, w_ada, b_ada, g_norm, w_qkv_a, rel_bias_a, w_o_a, w_ada_kv, b_ada_kv,
              g_kv, w_kv, w_f, b_f, w_q_b, w_o_b, w_pq, sub_keys, peer_u, peer_v, g_final):
    y_prompt, ak_p, av_p, bk_p, bv_p, bl_p = trunk(
        x_prompt, c_prompt, None, None, None, None, None,
        w_ada, b_ada, g_norm, w_qkv_a, rel_bias_a, w_o_a, w_ada_kv, b_ada_kv, g_kv,
        w_kv, w_f, b_f, w_q_b, w_o_b, w_pq, sub_keys, peer_u, peer_v, g_final)
    y_sample, ak_s, av_s, bk_s, bv_s, bl_s = trunk(
        x_sample, c_sample, cache_a_k, cache_a_v, cache_b_k, cache_b_v, cache_b_logf,
        w_ada, b_ada, g_norm, w_qkv_a, rel_bias_a, w_o_a, w_ada_kv, b_ada_kv, g_kv,
        w_kv, w_f, b_f, w_q_b, w_o_b, w_pq, sub_keys, peer_u, peer_v, g_final)
    return (y_prompt, y_sample, ak_p, av_p, bk_p, bv_p, bl_p, ak_s, av_s, bk_s, bv_s, bl_s)
```

```python
import functools

import jax
import jax.numpy as jnp
from jax import lax
from jax.experimental import pallas as pl
from jax.experimental.pallas import tpu as pltpu

F32 = jnp.float32
BF16 = jnp.bfloat16

D_MODEL = 1024
HEAD_DIM = 64
N_HEADS = D_MODEL // HEAD_DIM
HEAD_LANES = 128
CHUNK = 64
BAND_CHUNKS = 8
BAND = BAND_CHUNKS * CHUNK
REL_CLIP = 128
PEER_HEADS = 8
N_KEYS = 128
PEER_TOPK = 16
EPS = 1e-6
NEG_INF = -1e30

VMEM_LIMIT = 56 * 1024 * 1024


def _cparams(sem, vmem=VMEM_LIMIT):
    return pltpu.CompilerParams(dimension_semantics=sem, vmem_limit_bytes=vmem)


def _dot(a, b):
    return jnp.dot(a, b, preferred_element_type=F32)


def _dot_nt(a, b):
    return lax.dot_general(a, b, (((1,), (1,)), ((), ())), preferred_element_type=F32)


def _split3(x):
    hi = x.astype(BF16)
    r = x - hi.astype(F32)
    mid = r.astype(BF16)
    lo = (r - mid.astype(F32)).astype(BF16)
    return hi, mid, lo


def _cond_kernel(c_ref, w_ref, b_ref, o_ref):
    ch, cm, cl = _split3(c_ref[...])
    wh, wm, wl = _split3(w_ref[...])
    acc = _dot(ch, wh) + (_dot(ch, wm) + _dot(cm, wh)) + (_dot(ch, wl) + _dot(cl, wh) + _dot(cm, wm))
    o_ref[...] = acc + b_ref[...]


def _cond_matmul(c, w, b):
    m, d = c.shape
    n = w.shape[1]
    bn = 1024
    return pl.pallas_call(
        _cond_kernel,
        grid=(n // bn,),
        in_specs=[pl.BlockSpec((m, d), lambda j: (0, 0)),
                  pl.BlockSpec((d, bn), lambda j: (0, j)),
                  pl.BlockSpec((1, bn), lambda j: (0, j))],
        out_specs=pl.BlockSpec((m, bn), lambda j: (0, j)),
        out_shape=jax.ShapeDtypeStruct((m, n), F32),
        compiler_params=_cparams(("arbitrary",)),
        name="cond_matmul",
    )(c, w, b.reshape(1, n))


def _mod_norm(x, g, shift, scale):
    y = x * lax.rsqrt(jnp.mean(x * x, axis=-1, keepdims=True) + EPS)
    return (y * g) * (1.0 + scale) + shift


def _log_sigmoid(x):
    return jnp.minimum(x, 0.0) - jnp.log1p(jnp.exp(-jnp.abs(x)))


def _normproj_kernel(x_ref, g_ref, sh_ref, sc_ref, w_ref, *rest, splits, logsig_last):
    if logsig_last:
        bl_ref, o_refs = rest[0], rest[1:]
    else:
        o_refs = rest
    h = _mod_norm(x_ref[0], g_ref[...], sh_ref[0], sc_ref[0])
    out = _dot(h.astype(BF16), w_ref[...])
    off = 0
    for k, (o_ref, n) in enumerate(zip(o_refs, splits)):
        piece = out[:, off:off + n]
        if logsig_last and k == len(splits) - 1:
            piece = _log_sigmoid(piece + bl_ref[...])
        o_ref[0] = piece
        off += n


def _norm_proj(x, g, shift, scale, w, splits, tt, bias_last=None):
    n_seq, s, d = x.shape
    n = w.shape[1]
    logsig_last = bias_last is not None
    vec = pl.BlockSpec((1, 1, d), lambda b, i: (b, 0, 0))
    in_specs = [pl.BlockSpec((1, tt, d), lambda b, i: (b, i, 0)),
                pl.BlockSpec((1, d), lambda b, i: (0, 0)), vec, vec,
                pl.BlockSpec((d, n), lambda b, i: (0, 0))]
    args = [x, g.reshape(1, d), shift.reshape(n_seq, 1, d), scale.reshape(n_seq, 1, d), w]
    if logsig_last:
        in_specs.append(pl.BlockSpec((1, splits[-1]), lambda b, i: (0, 0)))
        args.append(bias_last.reshape(1, splits[-1]))
    return pl.pallas_call(
        functools.partial(_normproj_kernel, splits=tuple(splits), logsig_last=logsig_last),
        grid=(n_seq, s // tt),
        in_specs=in_specs,
        out_specs=[pl.BlockSpec((1, tt, k), lambda b, i: (b, i, 0)) for k in splits],
        out_shape=[jax.ShapeDtypeStruct((n_seq, s, k), F32) for k in splits],
        compiler_params=_cparams(("parallel", "parallel")),
        name="norm_proj",
    )(*args)


def _modnorm_kernel(x_ref, g_ref, sh_ref, sc_ref, o_ref):
    o_ref[0] = _mod_norm(x_ref[0], g_ref[...], sh_ref[0], sc_ref[0]).astype(BF16)


def _mod_norm_call(x, g, shift, scale, tt):
    n_seq, s, d = x.shape
    vec = pl.BlockSpec((1, 1, d), lambda b, i: (b, 0, 0))
    return pl.pallas_call(
        _modnorm_kernel,
        grid=(n_seq, s // tt),
        in_specs=[pl.BlockSpec((1, tt, d), lambda b, i: (b, i, 0)),
                  pl.BlockSpec((1, d), lambda b, i: (0, 0)), vec, vec],
        out_specs=pl.BlockSpec((1, tt, d), lambda b, i: (b, i, 0)),
        out_shape=jax.ShapeDtypeStruct((n_seq, s, d), BF16),
        compiler_params=_cparams(("parallel", "parallel")),
        name="mod_norm",
    )(x, g.reshape(1, d), shift.reshape(n_seq, 1, d), scale.reshape(n_seq, 1, d))


def _toheads_kernel(x_ref, *rest, scale, mode):
    if mode == "plain":
        (o_ref,) = rest
    else:
        cum_ref, o_ref = rest
    x = x_ref[0]
    tt = x.shape[0]
    lane = lax.broadcasted_iota(jnp.int32, (tt, HEAD_LANES), 1)
    for pair in range(N_HEADS // 2):
        col = x[:, pair * HEAD_LANES:(pair + 1) * HEAD_LANES]
        for par in range(2):
            h = 2 * pair + par
            c = col if par == 0 else pltpu.roll(col, HEAD_DIM, 1)
            if scale != 1.0:
                c = c * scale
            if mode == "plain":
                aug = jnp.zeros_like(c)
            else:
                cc = cum_ref[0][:, h:h + 1]
                c1 = cc.astype(BF16).astype(F32)
                r = cc - c1
                c2 = r.astype(BF16).astype(F32)
                c3 = r - c2
                one = jnp.ones_like(c)
                if mode == "q_aug":
                    aug = jnp.where(lane == 64, c1, jnp.where(lane == 65, c2, jnp.where(lane == 66, c3, one)))
                else:
                    aug = jnp.where(lane == 67, -c1, jnp.where(lane == 68, -c2, jnp.where(lane == 69, -c3, one)))
                aug = jnp.where(lane < 70, aug, 0.0)
            o_ref[0, h] = jnp.where(lane < HEAD_DIM, c, aug).astype(BF16)


def _to_heads(x, tt, scale=1.0, mode="plain", cum=None):
    n_seq, s, d = x.shape
    in_specs = [pl.BlockSpec((1, tt, d), lambda b, i: (b, i, 0))]
    args = [x]
    if mode != "plain":
        in_specs.append(pl.BlockSpec((1, tt, HEAD_LANES), lambda b, i: (b, i, 0)))
        args.append(cum)
    return pl.pallas_call(
        functools.partial(_toheads_kernel, scale=scale, mode=mode),
        grid=(n_seq, s // tt),
        in_specs=in_specs,
        out_specs=pl.BlockSpec((1, N_HEADS, tt, HEAD_LANES), lambda b, i: (b, 0, i, 0)),
        out_shape=jax.ShapeDtypeStruct((n_seq, N_HEADS, s, HEAD_LANES), BF16),
        compiler_params=_cparams(("parallel", "parallel")),
        name="to_heads",
    )(*args)


def _relbias_kernel(rb_ref, o_ref, *, n_q, n_k, split, band_mask):
    h = pl.program_id(0)
    lanes = o_ref.shape[2]
    idx = lax.broadcasted_iota(jnp.int32, (8, lanes), 1)
    dd = jnp.where(idx > split, BAND + (lanes - idx), BAND - idx)
    v = jnp.clip(dd, -REL_CLIP, REL_CLIP) + REL_CLIP

    def fill(k, t):
        return jnp.where(v == k, rb_ref[h, k], t)

    t = lax.fori_loop(0, 2 * REL_CLIP + 1, fill, jnp.zeros((8, lanes), F32))
    x = jnp.broadcast_to(t[0:1, :], (n_q, lanes))
    row = lax.broadcasted_iota(jnp.int32, (n_q, lanes), 0)
    bit = 1
    while bit < n_q:
        x = jnp.where((row & bit) != 0, pltpu.roll(x, bit, 1), x)
        bit *= 2
    if band_mask:
        col = lax.broadcasted_iota(jnp.int32, (n_q, lanes), 1)
        dc = BAND_CHUNKS + (row >> 6) - (col >> 6)
        x = jnp.where((dc >= 0) & (dc <= BAND_CHUNKS), x, NEG_INF)
    o_ref[0] = x


def _relbias_table(rel_bias, n_q, n_k, split, band_mask):
    lanes = -(-n_k // 128) * 128
    return pl.pallas_call(
        functools.partial(_relbias_kernel, n_q=n_q, n_k=n_k, split=split, band_mask=band_mask),
        grid=(N_HEADS,),
        in_specs=[pl.BlockSpec(memory_space=pltpu.SMEM)],
        out_specs=pl.BlockSpec((1, n_q, lanes), lambda h: (h, 0, 0)),
        out_shape=jax.ShapeDtypeStruct((N_HEADS, n_q, lanes), F32),
        compiler_params=_cparams(("arbitrary",)),
        name="relbias_table",
    )(rel_bias)


def _band_kernel(*refs, n_seg, first_seg_is_pad_at_zero):
    q_ref = refs[0]
    k_refs = refs[1:1 + n_seg]
    v_refs = refs[1 + n_seg:1 + 2 * n_seg]
    bias_ref = refs[1 + 2 * n_seg]
    o_ref = refs[2 + 2 * n_seg]
    q = q_ref[0, 0]
    scores = []
    off = 0
    for i in range(n_seg):
        k = k_refs[i][0, 0]
        nk = k.shape[0]
        s = _dot_nt(q, k) + bias_ref[0, :, off:off + nk]
        if i == 0 and first_seg_is_pad_at_zero:
            s = jnp.where(pl.program_id(2) > 0, s, NEG_INF)
        scores.append(s)
        off += nk
    m = functools.reduce(jnp.maximum, [jnp.max(s, axis=-1, keepdims=True) for s in scores])
    ps = [jnp.exp(s - m) for s in scores]
    l = functools.reduce(jnp.add, [jnp.sum(p, axis=-1, keepdims=True) for p in ps])
    acc = functools.reduce(jnp.add, [_dot(p.astype(BF16), v_ref[0, 0]) for p, v_ref in zip(ps, v_refs)])
    o_ref[0, 0] = (acc / l).astype(BF16)


def _band_attend_prompt(qh, kh, vh, bias):
    n_seq, nh, s, hl = qh.shape
    tg = BAND
    blk = (1, 1, tg, hl)
    cur = lambda h, b, g: (b, h, g, 0)
    prev = lambda h, b, g: (b, h, jnp.maximum(g - 1, 0), 0)
    return pl.pallas_call(
        functools.partial(_band_kernel, n_seg=2, first_seg_is_pad_at_zero=True),
        grid=(nh, n_seq, s // tg),
        in_specs=[pl.BlockSpec(blk, cur),
                  pl.BlockSpec(blk, prev), pl.BlockSpec(blk, cur),
                  pl.BlockSpec(blk, prev), pl.BlockSpec(blk, cur),
                  pl.BlockSpec((1, tg, 2 * tg), lambda h, b, g: (h, 0, 0))],
        out_specs=pl.BlockSpec(blk, cur),
        out_shape=jax.ShapeDtypeStruct(qh.shape, BF16),
        compiler_params=_cparams(("arbitrary", "arbitrary", "arbitrary")),
        name="band_attend_prompt",
    )(qh, kh, kh, vh, vh, bias)


def _band_attend_step(qh, kh, vh, bias):
    n_seq, nh, sq, hl = qh.shape
    sk = kh.shape[2]
    return pl.pallas_call(
        functools.partial(_band_kernel, n_seg=1, first_seg_is_pad_at_zero=False),
        grid=(nh, n_seq, 1),
        in_specs=[pl.BlockSpec((1, 1, sq, hl), lambda h, b, g: (b, h, 0, 0)),
                  pl.BlockSpec((1, 1, sk, hl), lambda h, b, g: (b, h, 0, 0)),
                  pl.BlockSpec((1, 1, sk, hl), lambda h, b, g: (b, h, 0, 0)),
                  pl.BlockSpec((1, sq, bias.shape[2]), lambda h, b, g: (h, 0, 0))],
        out_specs=pl.BlockSpec((1, 1, sq, hl), lambda h, b, g: (b, h, 0, 0)),
        out_shape=jax.ShapeDtypeStruct(qh.shape, BF16),
        compiler_params=_cparams(("arbitrary", "arbitrary", "arbitrary")),
        name="band_attend_step",
    )(qh, kh, vh, bias)


def _fox_kernel(q_ref, k_ref, v_ref, o_ref, *, tq, tk, past):
    qi = pl.program_id(2)
    q = q_ref[0, 0]

    def update(carry, s, v):
        m, l, acc = carry
        m_new = jnp.maximum(m, jnp.max(s, axis=-1, keepdims=True))
        alpha = jnp.exp(m - m_new)
        p = jnp.exp(s - m_new)
        l = alpha * l + jnp.sum(p, axis=-1, keepdims=True)
        acc = alpha * acc + _dot(p.astype(BF16), v)
        return m_new, l, acc

    def full_tile(j, carry):
        start = pl.multiple_of(j * tk, tk)
        k = k_ref[0, 0, pl.ds(start, tk), :]
        v = v_ref[0, 0, pl.ds(start, tk), :]
        return update(carry, _dot_nt(q, k), v)

    init = (jnp.full((tq, 1), NEG_INF, F32), jnp.zeros((tq, 1), F32), jnp.zeros((tq, HEAD_LANES), F32))
    q_start = past + qi * tq
    carry = lax.fori_loop(0, q_start // tk, full_tile, init)
    start = pl.multiple_of(q_start, tq)
    k = k_ref[0, 0, pl.ds(start, tq), :]
    v = v_ref[0, 0, pl.ds(start, tq), :]
    s = _dot_nt(q, k)
    row = lax.broadcasted_iota(jnp.int32, (tq, tq), 0)
    col = lax.broadcasted_iota(jnp.int32, (tq, tq), 1)
    s = jnp.where(col <= row, s, NEG_INF)
    m, l, acc = update(carry, s, v)
    o_ref[0, 0] = (acc / l).astype(BF16)


def _fox_attend(qh, kh, vh, tq, tk, past):
    n_seq, nh, sq, hl = qh.shape
    sk = kh.shape[2]
    return pl.pallas_call(
        functools.partial(_fox_kernel, tq=tq, tk=tk, past=past),
        grid=(n_seq, nh, sq // tq),
        in_specs=[pl.BlockSpec((1, 1, tq, hl), lambda b, h, i: (b, h, i, 0)),
                  pl.BlockSpec((1, 1, sk, hl), lambda b, h, i: (b, h, 0, 0)),
                  pl.BlockSpec((1, 1, sk, hl), lambda b, h, i: (b, h, 0, 0))],
        out_specs=pl.BlockSpec((1, 1, tq, hl), lambda b, h, i: (b, h, i, 0)),
        out_shape=jax.ShapeDtypeStruct(qh.shape, BF16),
        compiler_params=_cparams(("arbitrary", "arbitrary", "arbitrary")),
        name="fox_attend",
    )(qh, kh, vh)


def _cumsum_kernel(x_ref, o_ref, *, bs):
    n_blk = x_ref.shape[1] // bs
    r = lax.broadcasted_iota(jnp.int32, (bs, bs), 0)
    c = lax.broadcasted_iota(jnp.int32, (bs, bs), 1)
    tri = jnp.where(c <= r, 1.0, 0.0).astype(BF16)

    def body(i, carry):
        start = pl.multiple_of(i * bs, bs)
        hi, mid, lo = _split3(x_ref[0, pl.ds(start, bs), :])
        out = (_dot(tri, hi) + _dot(tri, mid) + _dot(tri, lo)) + carry
        o_ref[0, pl.ds(start, bs), :] = out
        return out[bs - 1:bs, :]

    lax.fori_loop(0, n_blk, body, jnp.zeros((1, x_ref.shape[2]), F32))


def _cumsum_seq(x, bs):
    n_seq, s, w = x.shape
    return pl.pallas_call(
        functools.partial(_cumsum_kernel, bs=bs),
        grid=(n_seq,),
        in_specs=[pl.BlockSpec((1, s, w), lambda b: (b, 0, 0))],
        out_specs=pl.BlockSpec((1, s, w), lambda b: (b, 0, 0)),
        out_shape=jax.ShapeDtypeStruct(x.shape, F32),
        compiler_params=_cparams(("parallel",)),
        name="cumsum_seq",
    )(x)


def _oproj_kernel(o_ref, w_ref, x_ref, gate_ref, out_ref):
    acc = _dot(o_ref[0, 0], w_ref[0])
    for h in range(1, N_HEADS):
        acc = acc + _dot(o_ref[0, h], w_ref[h])
    out_ref[0] = x_ref[0] + gate_ref[0] * acc


def _oproj_residual(oh, w_heads, x, gate, tt):
    n_seq, s, d = x.shape
    return pl.pallas_call(
        _oproj_kernel,
        grid=(n_seq, s // tt),
        in_specs=[pl.BlockSpec((1, N_HEADS, tt, HEAD_LANES), lambda b, i: (b, 0, i, 0)),
                  pl.BlockSpec((N_HEADS, HEAD_LANES, d), lambda b, i: (0, 0, 0)),
                  pl.BlockSpec((1, tt, d), lambda b, i: (b, i, 0)),
                  pl.BlockSpec((1, 1, d), lambda b, i: (b, 0, 0))],
        out_specs=pl.BlockSpec((1, tt, d), lambda b, i: (b, i, 0)),
        out_shape=jax.ShapeDtypeStruct(x.shape, F32),
        compiler_params=_cparams(("parallel", "parallel")),
        name="oproj_residual",
    )(oh, w_heads, x, gate.reshape(n_seq, 1, d))


LANE_CHUNK = 128


def _topk_rows(s, row):
    rank = jnp.full(s.shape, float(PEER_TOPK), F32)
    tops = []
    for r in range(PEER_TOPK):
        m = jnp.max(s, axis=0, keepdims=True)
        first = jnp.min(jnp.where(s == m, row, float(N_KEYS)), axis=0, keepdims=True)
        sel = row == first
        rank = jnp.where(sel, float(r), rank)
        s = jnp.where(sel, -jnp.inf, s)
        tops.append(m)
    return rank, tops


def _stack16(rows, row16):
    out = jnp.zeros(row16.shape, F32)
    for b, v in enumerate(rows):
        out = jnp.where(row16 == float(b), v, out)
    return out


def _select_pairs(top1, top2, row8):
    lanes = top1[0].shape
    row16 = jnp.concatenate([row8, row8 + 8.0], axis=0)
    t2 = _stack16(top2, row16)
    t2_lo, t2_hi = t2[0:8], t2[8:16]
    cands, poss = [], []
    for a in range(PEER_TOPK):
        nb = PEER_TOPK // (a + 1)
        halves = [(t2_lo, 0)] + ([(t2_hi, 8)] if nb > 8 else [])
        for t2h, b0 in halves:
            c = top1[a] + t2h
            valid = (row8 + float(b0)) < float(nb)
            cands.append(jnp.where(valid, c, -jnp.inf))
            poss.append(jnp.where(valid, row8 + float(a * PEER_TOPK + b0), 1e9))
    orig = list(cands)
    picked = [jnp.zeros(c.shape, F32) for c in cands]
    for _ in range(PEER_TOPK):
        m = jnp.max(functools.reduce(jnp.maximum, cands), axis=0, keepdims=True)
        hit = [jnp.where(c == m, p, 1e9) for c, p in zip(cands, poss)]
        first = jnp.min(functools.reduce(jnp.minimum, hit), axis=0, keepdims=True)
        sels = [p == first for p in poss]
        cands = [jnp.where(sl, -jnp.inf, c) for sl, c in zip(sels, cands)]
        picked = [jnp.where(sl, 1.0, pk) for sl, pk in zip(sels, picked)]
    m0 = top1[0] + top2[0]
    z = jnp.zeros(lanes, F32)
    n_sel = []
    k = 0
    for a in range(PEER_TOPK):
        nb = PEER_TOPK // (a + 1)
        n_a = jnp.zeros(lanes, F32)
        for _ in range(2 if nb > 8 else 1):
            n_a = n_a + jnp.sum(picked[k], axis=0, keepdims=True)
            e = jnp.where(picked[k] > 0.5, jnp.exp(orig[k] - m0), 0.0)
            z = z + jnp.sum(e, axis=0, keepdims=True)
            k += 1
        n_sel.append(n_a)
    return n_sel, z


def _gelu2(x):
    return x * (1.0 + jnp.tanh(0.7978845608028654 * (x + 0.044715 * (x * x * x))))


def _peer_kernel(h_ref, wq_ref, keys_ref, u_ref, vt_ref, out_ref,
                 xt_s, acc_s, ht_s, at_s, sc_s, f_s, r2_s, e1_s, n_s, *, tt, eb):
    e = pl.program_id(1)
    n_chunks = tt // LANE_CHUNK
    rows_per_blk = eb // N_KEYS

    @pl.when(e == 0)
    def _retrieve():
        xt = h_ref[...].astype(F32).T.astype(BF16)
        xt_s[...] = xt
        acc_s[...] = jnp.zeros_like(acc_s)
        qt = _dot(wq_ref[...], xt)
        for hp in range(2 * PEER_HEADS):
            sc_s[hp] = _dot(keys_ref[hp], qt[hp * N_KEYS:(hp + 1) * N_KEYS].astype(BF16))

        def per_head(hc, _):
            h = hc // n_chunks
            c0 = pl.multiple_of((hc % n_chunks) * LANE_CHUNK, LANE_CHUNK)
            lanes = pl.ds(c0, LANE_CHUNK)
            row = lax.broadcasted_iota(jnp.int32, (N_KEYS, LANE_CHUNK), 0).astype(F32)
            row8 = lax.broadcasted_iota(jnp.int32, (8, LANE_CHUNK), 0).astype(F32)
            s1 = sc_s[2 * h, :, lanes]
            s2 = sc_s[2 * h + 1, :, lanes]
            rank1, top1 = _topk_rows(s1, row)
            rank2, top2 = _topk_rows(s2, row)
            n_sel, z = _select_pairs(top1, top2, row8)
            n_dense = jnp.zeros((N_KEYS, LANE_CHUNK), F32)
            for a in range(PEER_TOPK):
                n_dense = jnp.where(rank1 == float(a), n_sel[a], n_dense)
            n_s[h, :, lanes] = n_dense
            e1_s[h, :, lanes] = jnp.exp(s1 - top1[0])
            f_s[h, :, lanes] = jnp.exp(s2 - top2[0]) * (0.5 / z)
            r2_s[h, :, lanes] = rank2
            return 0

        lax.fori_loop(0, PEER_HEADS * n_chunks, per_head, 0)

    ht_s[...] = _dot(u_ref[...], xt_s[...])

    def per_chunk(c, _):
        lanes = pl.ds(pl.multiple_of(c * LANE_CHUNK, LANE_CHUNK), LANE_CHUNK)
        grp = pl.ds(pl.multiple_of(e * rows_per_blk, rows_per_blk), rows_per_blk)
        n_grp = [n_s[h, grp, lanes] for h in range(PEER_HEADS)]
        e_grp = [e1_s[h, grp, lanes] for h in range(PEER_HEADS)]
        for il in range(rows_per_blk):
            w = jnp.zeros((N_KEYS, LANE_CHUNK), F32)
            for h in range(PEER_HEADS):
                picked = r2_s[h, :, lanes] < n_grp[h][il:il + 1]
                w = w + jnp.where(picked, f_s[h, :, lanes], 0.0) * e_grp[h][il:il + 1]
            rows = slice(il * N_KEYS, (il + 1) * N_KEYS)
            at_s[rows, lanes] = (_gelu2(ht_s[rows, lanes]) * w).astype(BF16)
        return 0

    lax.fori_loop(0, n_chunks, per_chunk, 0)
    acc_s[...] += _dot(vt_ref[...], at_s[...])

    @pl.when(e == pl.num_programs(1) - 1)
    def _emit():
        out_ref[...] = acc_s[...].T


def _peer(h_flat, wq_t, keys, u, v_t, tt, eb):
    t, d = h_flat.shape
    n_exp = u.shape[0]
    scr = pltpu.VMEM
    return pl.pallas_call(
        functools.partial(_peer_kernel, tt=tt, eb=eb),
        grid=(t // tt, n_exp // eb),
        in_specs=[pl.BlockSpec((tt, d), lambda i, e: (i, 0)),
                  pl.BlockSpec(wq_t.shape, lambda i, e: (0, 0)),
                  pl.BlockSpec(keys.shape, lambda i, e: (0, 0, 0)),
                  pl.BlockSpec((eb, d), lambda i, e: (e, 0)),
                  pl.BlockSpec((d, eb), lambda i, e: (0, e))],
        out_specs=pl.BlockSpec((tt, d), lambda i, e: (i, 0)),
        out_shape=jax.ShapeDtypeStruct((t, d), F32),
        scratch_shapes=[scr((d, tt), BF16), scr((d, tt), F32), scr((eb, tt), F32), scr((eb, tt), BF16),
                        scr((2 * PEER_HEADS, N_KEYS, tt), F32),
                        scr((PEER_HEADS, N_KEYS, tt), F32), scr((PEER_HEADS, N_KEYS, tt), F32),
                        scr((PEER_HEADS, N_KEYS, tt), F32), scr((PEER_HEADS, N_KEYS, tt), F32)],
        compiler_params=_cparams(("parallel", "arbitrary")),
        name="peer_dense",
    )(h_flat, wq_t, keys, u, v_t)


def _residual_kernel(x_ref, gate_ref, o_ref, *rest, final):
    x = x_ref[0] + gate_ref[0] * o_ref[0]
    if final:
        g_ref, out_ref = rest
        out_ref[0] = (x * lax.rsqrt(jnp.mean(x * x, axis=-1, keepdims=True) + EPS)) * g_ref[...]
    else:
        (out_ref,) = rest
        out_ref[0] = x


def _residual(x, gate, o, tt, g_final=None):
    n_seq, s, d = x.shape
    final = g_final is not None
    blk = pl.BlockSpec((1, tt, d), lambda b, i: (b, i, 0))
    in_specs = [blk, pl.BlockSpec((1, 1, d), lambda b, i: (b, 0, 0)), blk]
    args = [x, gate.reshape(n_seq, 1, d), o]
    if final:
        in_specs.append(pl.BlockSpec((1, d), lambda b, i: (0, 0)))
        args.append(g_final.reshape(1, d))
    return pl.pallas_call(
        functools.partial(_residual_kernel, final=final),
        grid=(n_seq, s // tt),
        in_specs=in_specs,
        out_specs=blk,
        out_shape=jax.ShapeDtypeStruct(x.shape, F32),
        compiler_params=_cparams(("parallel", "parallel")),
        name="residual",
    )(*args)


def _row_tile(s):
    return 256 if s % 256 == 0 else s


def _peer_tile(t):
    return 512 if t % 512 == 0 else 256


def _pad_lanes(x, n):
    return jnp.pad(x, [(0, 0)] * (x.ndim - 1) + [(0, n - x.shape[-1])])


def _prepare_weights(w_qkv_a, w_o_a, w_kv, w_f, b_f, w_q_b, w_o_b, w_pq, sub_keys, peer_u, peer_v):
    depth = w_pq.shape[0]

    def o_heads(w):
        w = w.reshape(N_HEADS, HEAD_DIM, D_MODEL)
        return jnp.pad(w, ((0, 0), (0, HEAD_LANES - HEAD_DIM), (0, 0))).astype(BF16)

    return dict(
        w_qkv=w_qkv_a[0].astype(BF16),
        w_o_a=o_heads(w_o_a[0]),
        w_kvf=jnp.concatenate([w_kv, _pad_lanes(w_f, HEAD_LANES)], axis=1).astype(BF16),
        b_f=_pad_lanes(b_f, HEAD_LANES),
        w_q_b=w_q_b[0].astype(BF16),
        w_o_b=o_heads(w_o_b[0]),
        w_pq_t=[w_pq[l].T.astype(BF16) for l in range(depth)],
        keys=[jnp.swapaxes(sub_keys[l], 0, 1).reshape(2 * PEER_HEADS, N_KEYS, -1).astype(BF16)
              for l in range(depth)],
        u=[peer_u[l].astype(BF16) for l in range(depth)],
        v_t=[peer_v[l].T.astype(BF16) for l in range(depth)],
    )


def _peer_layer(x, g, shift, scale, gate, wts, l, g_final=None):
    n_seq, s, d = x.shape
    tt = _row_tile(s)
    h = _mod_norm_call(x, g, shift, scale, tt).reshape(n_seq * s, d)
    o = _peer(h, wts["w_pq_t"][l], wts["keys"][l], wts["u"][l], wts["v_t"][l],
              tt=_peer_tile(n_seq * s), eb=1024)
    return _residual(x, gate, o.reshape(n_seq, s, d), tt, g_final=g_final)


def _trunk(x, mods0, mods1, modkv, caches, wts, g_norm, g_kv, g_final, rel_bias):
    n_seq, s, d = x.shape
    tt = _row_tile(s)
    sh_m, sc_m, gt_m, sh_c, sc_c, gt_c = jnp.split(mods0, 6, axis=-1)

    q, k, v = _norm_proj(x, g_norm[0, 0], sh_m, sc_m, wts["w_qkv"], (d, d, d), tt)
    qh = _to_heads(q, tt, scale=HEAD_DIM ** -0.5)
    if caches is None:
        kh, vh = _to_heads(k, tt), _to_heads(v, tt)
        bias = _relbias_table(rel_bias, BAND, 2 * BAND, split=768, band_mask=True)
        oh = _band_attend_prompt(qh, kh, vh, bias)
        keep = min(BAND, s)
        new_ak, new_av = k[:, s - keep:], v[:, s - keep:]
    else:
        cache_a_k, cache_a_v = caches[0], caches[1]
        win = cache_a_k.shape[2]
        k_all = jnp.concatenate([cache_a_k[0].reshape(n_seq, win, d), k], axis=1)
        v_all = jnp.concatenate([cache_a_v[0].reshape(n_seq, win, d), v], axis=1)
        kh, vh = _to_heads(k_all, win + s), _to_heads(v_all, win + s)
        lanes = -(-(win + s) // 128) * 128
        bias = _relbias_table(rel_bias, s, win + s, split=(win + s + lanes - s) // 2, band_mask=False)
        oh = _band_attend_step(qh, kh, vh, bias)
        new_ak, new_av = k, v
    x = _oproj_residual(oh, wts["w_o_a"], x, gt_m, tt)
    x = _peer_layer(x, g_norm[0, 1], sh_c, sc_c, gt_c, wts, 0)

    sh_m, sc_m, gt_m, sh_c, sc_c, gt_c = jnp.split(mods1, 6, axis=-1)
    sh_kv, sc_kv = jnp.split(modkv, 2, axis=-1)
    bk, bv, lf = _norm_proj(x, g_kv, sh_kv, sc_kv, wts["w_kvf"], (d, d, HEAD_LANES), tt, bias_last=wts["b_f"])
    (q,) = _norm_proj(x, g_norm[1, 0], sh_m, sc_m, wts["w_q_b"], (d,), tt)
    if caches is None:
        past = 0
        k_all, v_all, lf_all = bk, bv, lf
        tq = tk = 512 if s % 512 == 0 else s
        bs = 256 if s % 256 == 0 else s
    else:
        cache_b_k, cache_b_v, cache_b_logf = caches[2], caches[3], caches[4]
        past = cache_b_k.shape[1]
        k_all = jnp.concatenate([cache_b_k.reshape(n_seq, past, d), bk], axis=1)
        v_all = jnp.concatenate([cache_b_v.reshape(n_seq, past, d), bv], axis=1)
        lf_all = jnp.concatenate([_pad_lanes(cache_b_logf.astype(F32), HEAD_LANES), lf], axis=1)
        tq, tk = s, 512
        bs = s
    cum = _cumsum_seq(lf_all, bs)
    tk_rows = _row_tile(past + s) if (past + s) % 256 == 0 else s
    qh = _to_heads(q, tt, scale=HEAD_DIM ** -0.5, mode="q_aug", cum=cum[:, past:])
    kh = _to_heads(k_all, tk_rows, mode="k_aug", cum=cum)
    vh = _to_heads(v_all, tk_rows)
    oh = _fox_attend(qh, kh, vh, tq, tk, past)
    x = _oproj_residual(oh, wts["w_o_b"], x, gt_m, tt)
    y = _peer_layer(x, g_norm[1, 1], sh_c, sc_c, gt_c, wts, 1, g_final=g_final)

    hshape = (n_seq, -1, N_HEADS, HEAD_DIM)
    return (y, new_ak.reshape(hshape)[None], new_av.reshape(hshape)[None],
            bk.reshape(hshape), bv.reshape(hshape), lf[:, :, :N_HEADS])


def kernel(x_prompt, x_sample, c_prompt, c_sample, cache_a_k, cache_a_v, cache_b_k, cache_b_v, cache_b_logf, w_ada, b_ada, g_norm, w_qkv_a, rel_bias_a, w_o_a, w_ada_kv, b_ada_kv, g_kv, w_kv, w_f, b_f, w_q_b, w_o_b, w_pq, sub_keys, peer_u, peer_v, g_final):
    n_p, n_s = c_prompt.shape[0], c_sample.shape[0]
    n_c = -(-(n_p + n_s) // 16) * 16
    c_all = jnp.pad(jnp.concatenate([c_prompt, c_sample], axis=0), ((0, n_c - n_p - n_s), (0, 0)))
    mods0 = _cond_matmul(c_all, w_ada[0], b_ada[0])
    mods1 = _cond_matmul(c_all, w_ada[1], b_ada[1])
    modkv = _cond_matmul(c_all, w_ada_kv, b_ada_kv)
    wts = _prepare_weights(w_qkv_a, w_o_a, w_kv, w_f, b_f, w_q_b, w_o_b, w_pq, sub_keys, peer_u, peer_v)

    out_p = _trunk(x_prompt, mods0[:n_p], mods1[:n_p], modkv[:n_p], None,
                   wts, g_norm, g_kv, g_final, rel_bias_a[0])
    out_s = _trunk(x_sample, mods0[n_p:n_p + n_s], mods1[n_p:n_p + n_s], modkv[n_p:n_p + n_s],
                   (cache_a_k, cache_a_v, cache_b_k, cache_b_v, cache_b_logf),
                   wts, g_norm, g_kv, g_final, rel_bias_a[0])
    return (out_p[0], out_s[0]) + out_p[1:] + out_s[1:]
```

```python
import functools

import jax
import jax.numpy as jnp
from jax import lax
from jax.experimental import pallas as pl
from jax.experimental.pallas import tpu as pltpu

F32 = jnp.float32
BF16 = jnp.bfloat16

D_MODEL = 1024
HEAD_DIM = 64
N_HEADS = D_MODEL // HEAD_DIM
HEAD_LANES = 128
HEADS_PER_STEP = 2
CHUNK = 64
BAND_CHUNKS = 8
BAND = BAND_CHUNKS * CHUNK
REL_CLIP = 128
PEER_HEADS = 8
N_KEYS = 128
PEER_TOPK = 16
EPS = 1e-6
NEG_INF = -1e30

VMEM_LIMIT = 56 * 1024 * 1024


def _cparams(sem, vmem=VMEM_LIMIT):
    return pltpu.CompilerParams(dimension_semantics=sem, vmem_limit_bytes=vmem)


def _dot(a, b):
    return jnp.dot(a, b, preferred_element_type=F32)


def _dot_nt(a, b):
    return lax.dot_general(a, b, (((1,), (1,)), ((), ())), preferred_element_type=F32)


def _split3(x):
    hi = x.astype(BF16)
    r = x - hi.astype(F32)
    mid = r.astype(BF16)
    lo = (r - mid.astype(F32)).astype(BF16)
    return hi, mid, lo


def _cond_kernel(c_ref, w_ref, b_ref, o_ref):
    ch, cm, cl = _split3(c_ref[...])
    wh, wm, wl = _split3(w_ref[...])
    acc = _dot(ch, wh) + (_dot(ch, wm) + _dot(cm, wh)) + (_dot(ch, wl) + _dot(cl, wh) + _dot(cm, wm))
    o_ref[...] = acc + b_ref[...]


def _cond_matmul(c, w, b):
    m, d = c.shape
    n = w.shape[1]
    bn = 1024
    return pl.pallas_call(
        _cond_kernel,
        grid=(n // bn,),
        in_specs=[pl.BlockSpec((m, d), lambda j: (0, 0)),
                  pl.BlockSpec((d, bn), lambda j: (0, j)),
                  pl.BlockSpec((1, bn), lambda j: (0, j))],
        out_specs=pl.BlockSpec((m, bn), lambda j: (0, j)),
        out_shape=jax.ShapeDtypeStruct((m, n), F32),
        compiler_params=_cparams(("arbitrary",)),
        name="cond_matmul",
    )(c, w, b.reshape(1, n))


def _mod_norm(x, g, shift, scale):
    y = x * lax.rsqrt(jnp.mean(x * x, axis=-1, keepdims=True) + EPS)
    return (y * g) * (1.0 + scale) + shift


def _log_sigmoid(x):
    return jnp.minimum(x, 0.0) - jnp.log1p(jnp.exp(-jnp.abs(x)))


def _normproj_kernel(x_ref, g_ref, sh_ref, sc_ref, w_ref, *rest, splits, logsig_last):
    if logsig_last:
        bl_ref, o_refs = rest[0], rest[1:]
    else:
        o_refs = rest
    h = _mod_norm(x_ref[0], g_ref[...], sh_ref[0], sc_ref[0])
    out = _dot(h.astype(BF16), w_ref[...])
    off = 0
    for k, (o_ref, n) in enumerate(zip(o_refs, splits)):
        piece = out[:, off:off + n]
        if logsig_last and k == len(splits) - 1:
            piece = _log_sigmoid(piece + bl_ref[...])
        o_ref[0] = piece
        off += n


def _norm_proj(x, g, shift, scale, w, splits, tt, bias_last=None):
    n_seq, s, d = x.shape
    n = w.shape[1]
    logsig_last = bias_last is not None
    vec = pl.BlockSpec((1, 1, d), lambda b, i: (b, 0, 0))
    in_specs = [pl.BlockSpec((1, tt, d), lambda b, i: (b, i, 0)),
                pl.BlockSpec((1, d), lambda b, i: (0, 0)), vec, vec,
                pl.BlockSpec((d, n), lambda b, i: (0, 0))]
    args = [x, g.reshape(1, d), shift.reshape(n_seq, 1, d), scale.reshape(n_seq, 1, d), w]
    if logsig_last:
        in_specs.append(pl.BlockSpec((1, splits[-1]), lambda b, i: (0, 0)))
        args.append(bias_last.reshape(1, splits[-1]))
    return pl.pallas_call(
        functools.partial(_normproj_kernel, splits=tuple(splits), logsig_last=logsig_last),
        grid=(n_seq, s // tt),
        in_specs=in_specs,
        out_specs=[pl.BlockSpec((1, tt, k), lambda b, i: (b, i, 0)) for k in splits],
        out_shape=[jax.ShapeDtypeStruct((n_seq, s, k), F32) for k in splits],
        compiler_params=_cparams(("parallel", "parallel")),
        name="norm_proj",
    )(*args)


def _modnorm_kernel(x_ref, g_ref, sh_ref, sc_ref, o_ref):
    o_ref[0] = _mod_norm(x_ref[0], g_ref[...], sh_ref[0], sc_ref[0]).astype(BF16)


def _mod_norm_call(x, g, shift, scale, tt):
    n_seq, s, d = x.shape
    vec = pl.BlockSpec((1, 1, d), lambda b, i: (b, 0, 0))
    return pl.pallas_call(
        _modnorm_kernel,
        grid=(n_seq, s // tt),
        in_specs=[pl.BlockSpec((1, tt, d), lambda b, i: (b, i, 0)),
                  pl.BlockSpec((1, d), lambda b, i: (0, 0)), vec, vec],
        out_specs=pl.BlockSpec((1, tt, d), lambda b, i: (b, i, 0)),
        out_shape=jax.ShapeDtypeStruct((n_seq, s, d), BF16),
        compiler_params=_cparams(("parallel", "parallel")),
        name="mod_norm",
    )(x, g.reshape(1, d), shift.reshape(n_seq, 1, d), scale.reshape(n_seq, 1, d))


def _toheads_kernel(x_ref, *rest, scale, mode):
    if mode == "plain":
        (o_ref,) = rest
    else:
        cum_ref, o_ref = rest
    x = x_ref[0]
    tt = x.shape[0]
    lane = lax.broadcasted_iota(jnp.int32, (tt, HEAD_LANES), 1)
    for pair in range(N_HEADS // 2):
        col = x[:, pair * HEAD_LANES:(pair + 1) * HEAD_LANES]
        for par in range(2):
            h = 2 * pair + par
            c = col if par == 0 else pltpu.roll(col, HEAD_DIM, 1)
            if scale != 1.0:
                c = c * scale
            if mode == "plain":
                aug = jnp.zeros_like(c)
            else:
                cc = cum_ref[0][:, h:h + 1]
                c1 = cc.astype(BF16).astype(F32)
                r = cc - c1
                c2 = r.astype(BF16).astype(F32)
                c3 = r - c2
                one = jnp.ones_like(c)
                if mode == "q_aug":
                    aug = jnp.where(lane == 64, c1, jnp.where(lane == 65, c2, jnp.where(lane == 66, c3, one)))
                else:
                    aug = jnp.where(lane == 67, -c1, jnp.where(lane == 68, -c2, jnp.where(lane == 69, -c3, one)))
                aug = jnp.where(lane < 70, aug, 0.0)
            o_ref[0, h] = jnp.where(lane < HEAD_DIM, c, aug).astype(BF16)


def _to_heads(x, tt, scale=1.0, mode="plain", cum=None):
    n_seq, s, d = x.shape
    in_specs = [pl.BlockSpec((1, tt, d), lambda b, i: (b, i, 0))]
    args = [x]
    if mode != "plain":
        in_specs.append(pl.BlockSpec((1, tt, HEAD_LANES), lambda b, i: (b, i, 0)))
        args.append(cum)
    return pl.pallas_call(
        functools.partial(_toheads_kernel, scale=scale, mode=mode),
        grid=(n_seq, s // tt),
        in_specs=in_specs,
        out_specs=pl.BlockSpec((1, N_HEADS, tt, HEAD_LANES), lambda b, i: (b, 0, i, 0)),
        out_shape=jax.ShapeDtypeStruct((n_seq, N_HEADS, s, HEAD_LANES), BF16),
        compiler_params=_cparams(("parallel", "parallel")),
        name="to_heads",
    )(*args)


def _relbias_kernel(rb_ref, o_ref, *, n_q, n_k, split, band_mask):
    h = pl.program_id(0)
    lanes = o_ref.shape[2]
    idx = lax.broadcasted_iota(jnp.int32, (8, lanes), 1)
    dd = jnp.where(idx > split, BAND + (lanes - idx), BAND - idx)
    v = jnp.clip(dd, -REL_CLIP, REL_CLIP) + REL_CLIP

    def fill(k, t):
        return jnp.where(v == k, rb_ref[h, k], t)

    t = lax.fori_loop(0, 2 * REL_CLIP + 1, fill, jnp.zeros((8, lanes), F32))
    x = jnp.broadcast_to(t[0:1, :], (n_q, lanes))
    row = lax.broadcasted_iota(jnp.int32, (n_q, lanes), 0)
    bit = 1
    while bit < n_q:
        x = jnp.where((row & bit) != 0, pltpu.roll(x, bit, 1), x)
        bit *= 2
    if band_mask:
        col = lax.broadcasted_iota(jnp.int32, (n_q, lanes), 1)
        dc = BAND_CHUNKS + (row >> 6) - (col >> 6)
        x = jnp.where((dc >= 0) & (dc <= BAND_CHUNKS), x, NEG_INF)
    o_ref[0] = x


def _relbias_table(rel_bias, n_q, n_k, split, band_mask):
    lanes = -(-n_k // 128) * 128
    return pl.pallas_call(
        functools.partial(_relbias_kernel, n_q=n_q, n_k=n_k, split=split, band_mask=band_mask),
        grid=(N_HEADS,),
        in_specs=[pl.BlockSpec(memory_space=pltpu.SMEM)],
        out_specs=pl.BlockSpec((1, n_q, lanes), lambda h: (h, 0, 0)),
        out_shape=jax.ShapeDtypeStruct((N_HEADS, n_q, lanes), F32),
        compiler_params=_cparams(("arbitrary",)),
        name="relbias_table",
    )(rel_bias)


def _band_kernel(*refs, n_seg, first_seg_is_pad_at_zero):
    q_ref = refs[0]
    k_refs = refs[1:1 + n_seg]
    v_refs = refs[1 + n_seg:1 + 2 * n_seg]
    bias_ref = refs[1 + 2 * n_seg]
    o_ref = refs[2 + 2 * n_seg]
    for j in range(HEADS_PER_STEP):
        q = q_ref[0, j]
        scores = []
        off = 0
        for i in range(n_seg):
            k = k_refs[i][0, j]
            nk = k.shape[0]
            s = _dot_nt(q, k) + bias_ref[j, :, off:off + nk]
            if i == 0 and first_seg_is_pad_at_zero:
                s = jnp.where(pl.program_id(2) > 0, s, NEG_INF)
            scores.append(s)
            off += nk
        m = functools.reduce(jnp.maximum, [jnp.max(s, axis=-1, keepdims=True) for s in scores])
        ps = [jnp.exp(s - m) for s in scores]
        l = functools.reduce(jnp.add, [jnp.sum(p, axis=-1, keepdims=True) for p in ps])
        acc = functools.reduce(jnp.add, [_dot(p.astype(BF16), v_ref[0, j]) for p, v_ref in zip(ps, v_refs)])
        o_ref[0, j] = (acc / l).astype(BF16)


def _band_attend_prompt(qh, kh, vh, bias):
    n_seq, nh, s, hl = qh.shape
    tg = BAND
    blk = (1, HEADS_PER_STEP, tg, hl)
    cur = lambda h, b, g: (b, h, g, 0)
    prev = lambda h, b, g: (b, h, jnp.maximum(g - 1, 0), 0)
    return pl.pallas_call(
        functools.partial(_band_kernel, n_seg=2, first_seg_is_pad_at_zero=True),
        grid=(nh // HEADS_PER_STEP, n_seq, s // tg),
        in_specs=[pl.BlockSpec(blk, cur),
                  pl.BlockSpec(blk, prev), pl.BlockSpec(blk, cur),
                  pl.BlockSpec(blk, prev), pl.BlockSpec(blk, cur),
                  pl.BlockSpec((HEADS_PER_STEP, tg, 2 * tg), lambda h, b, g: (h, 0, 0))],
        out_specs=pl.BlockSpec(blk, cur),
        out_shape=jax.ShapeDtypeStruct(qh.shape, BF16),
        compiler_params=_cparams(("arbitrary", "arbitrary", "arbitrary")),
        name="band_attend_prompt",
    )(qh, kh, kh, vh, vh, bias)


def _band_attend_step(qh, kh, vh, bias):
    n_seq, nh, sq, hl = qh.shape
    sk = kh.shape[2]
    hps = HEADS_PER_STEP
    return pl.pallas_call(
        functools.partial(_band_kernel, n_seg=1, first_seg_is_pad_at_zero=False),
        grid=(nh // hps, n_seq, 1),
        in_specs=[pl.BlockSpec((1, hps, sq, hl), lambda h, b, g: (b, h, 0, 0)),
                  pl.BlockSpec((1, hps, sk, hl), lambda h, b, g: (b, h, 0, 0)),
                  pl.BlockSpec((1, hps, sk, hl), lambda h, b, g: (b, h, 0, 0)),
                  pl.BlockSpec((hps, sq, bias.shape[2]), lambda h, b, g: (h, 0, 0))],
        out_specs=pl.BlockSpec((1, hps, sq, hl), lambda h, b, g: (b, h, 0, 0)),
        out_shape=jax.ShapeDtypeStruct(qh.shape, BF16),
        compiler_params=_cparams(("arbitrary", "arbitrary", "arbitrary")),
        name="band_attend_step",
    )(qh, kh, vh, bias)


def _fox_kernel(q_ref, k_ref, v_ref, o_ref, *, tq, tk, past):
    qi = pl.program_id(2)
    heads = range(HEADS_PER_STEP)
    qs = [q_ref[0, j] for j in heads]

    def update(carry, s, v):
        m, l, acc = carry
        m_new = jnp.maximum(m, jnp.max(s, axis=-1, keepdims=True))
        alpha = jnp.exp(m - m_new)
        p = jnp.exp(s - m_new)
        l = alpha * l + jnp.sum(p, axis=-1, keepdims=True)
        acc = alpha * acc + _dot(p.astype(BF16), v)
        return m_new, l, acc

    def full_tile(t, carries):
        rows = pl.ds(pl.multiple_of(t * tk, tk), tk)
        return tuple(update(carries[j], _dot_nt(qs[j], k_ref[0, j, rows, :]), v_ref[0, j, rows, :])
                     for j in heads)

    init = tuple((jnp.full((tq, 1), NEG_INF, F32), jnp.zeros((tq, 1), F32), jnp.zeros((tq, HEAD_LANES), F32))
                 for _ in heads)
    q_start = past + qi * tq
    carries = lax.fori_loop(0, q_start // tk, full_tile, init)
    rows = pl.ds(pl.multiple_of(q_start, tq), tq)
    row = lax.broadcasted_iota(jnp.int32, (tq, tq), 0)
    col = lax.broadcasted_iota(jnp.int32, (tq, tq), 1)
    for j in heads:
        s = jnp.where(col <= row, _dot_nt(qs[j], k_ref[0, j, rows, :]), NEG_INF)
        m, l, acc = update(carries[j], s, v_ref[0, j, rows, :])
        o_ref[0, j] = (acc / l).astype(BF16)


def _fox_attend(qh, kh, vh, tq, tk, past):
    n_seq, nh, sq, hl = qh.shape
    sk = kh.shape[2]
    hps = HEADS_PER_STEP
    return pl.pallas_call(
        functools.partial(_fox_kernel, tq=tq, tk=tk, past=past),
        grid=(n_seq, nh // hps, sq // tq),
        in_specs=[pl.BlockSpec((1, hps, tq, hl), lambda b, h, i: (b, h, i, 0)),
                  pl.BlockSpec((1, hps, sk, hl), lambda b, h, i: (b, h, 0, 0)),
                  pl.BlockSpec((1, hps, sk, hl), lambda b, h, i: (b, h, 0, 0))],
        out_specs=pl.BlockSpec((1, hps, tq, hl), lambda b, h, i: (b, h, i, 0)),
        out_shape=jax.ShapeDtypeStruct(qh.shape, BF16),
        compiler_params=_cparams(("arbitrary", "arbitrary", "arbitrary")),
        name="fox_attend",
    )(qh, kh, vh)


def _cumsum_kernel(x_ref, o_ref, *, bs):
    n_blk = x_ref.shape[1] // bs
    r = lax.broadcasted_iota(jnp.int32, (bs, bs), 0)
    c = lax.broadcasted_iota(jnp.int32, (bs, bs), 1)
    tri = jnp.where(c <= r, 1.0, 0.0).astype(BF16)

    def body(i, carry):
        start = pl.multiple_of(i * bs, bs)
        hi, mid, lo = _split3(x_ref[0, pl.ds(start, bs), :])
        out = (_dot(tri, hi) + _dot(tri, mid) + _dot(tri, lo)) + carry
        o_ref[0, pl.ds(start, bs), :] = out
        return out[bs - 1:bs, :]

    lax.fori_loop(0, n_blk, body, jnp.zeros((1, x_ref.shape[2]), F32))


def _cumsum_seq(x, bs):
    n_seq, s, w = x.shape
    return pl.pallas_call(
        functools.partial(_cumsum_kernel, bs=bs),
        grid=(n_seq,),
        in_specs=[pl.BlockSpec((1, s, w), lambda b: (b, 0, 0))],
        out_specs=pl.BlockSpec((1, s, w), lambda b: (b, 0, 0)),
        out_shape=jax.ShapeDtypeStruct(x.shape, F32),
        compiler_params=_cparams(("parallel",)),
        name="cumsum_seq",
    )(x)


def _oproj_kernel(o_ref, w_ref, x_ref, gate_ref, out_ref):
    acc = _dot(o_ref[0, 0], w_ref[0])
    for h in range(1, N_HEADS):
        acc = acc + _dot(o_ref[0, h], w_ref[h])
    out_ref[0] = x_ref[0] + gate_ref[0] * acc


def _oproj_residual(oh, w_heads, x, gate, tt):
    n_seq, s, d = x.shape
    return pl.pallas_call(
        _oproj_kernel,
        grid=(n_seq, s // tt),
        in_specs=[pl.BlockSpec((1, N_HEADS, tt, HEAD_LANES), lambda b, i: (b, 0, i, 0)),
                  pl.BlockSpec((N_HEADS, HEAD_LANES, d), lambda b, i: (0, 0, 0)),
                  pl.BlockSpec((1, tt, d), lambda b, i: (b, i, 0)),
                  pl.BlockSpec((1, 1, d), lambda b, i: (b, 0, 0))],
        out_specs=pl.BlockSpec((1, tt, d), lambda b, i: (b, i, 0)),
        out_shape=jax.ShapeDtypeStruct(x.shape, F32),
        compiler_params=_cparams(("parallel", "parallel")),
        name="oproj_residual",
    )(oh, w_heads, x, gate.reshape(n_seq, 1, d))


LANE_CHUNK = 128
PACK_ROWS = 16


def _topk_rows(s, row, exact):
    rank = jnp.full(s.shape, float(PEER_TOPK), F32)
    tops = []
    for r in range(PEER_TOPK):
        m = jnp.max(s, axis=0, keepdims=True)
        sel = s == m
        if exact:
            first = jnp.min(jnp.where(sel, row, float(N_KEYS)), axis=0, keepdims=True)
            sel = row == first
        rank = jnp.where(sel, float(r), rank)
        s = jnp.where(sel, -jnp.inf, s)
        tops.append(m)
    count = jnp.sum(jnp.where(rank < float(PEER_TOPK), 1.0, 0.0), axis=0, keepdims=True)
    return rank, tops, count


def _stack16(rows, row16):
    out = jnp.zeros(row16.shape, F32)
    for b, v in enumerate(rows):
        out = jnp.where(row16 == float(b), v, out)
    return out


def _select_pairs(top1, top2, row8, exact):
    lanes = top1[0].shape
    row16 = jnp.concatenate([row8, row8 + 8.0], axis=0)
    t2 = _stack16(top2, row16)
    t2_lo, t2_hi = t2[0:8], t2[8:16]
    cands, poss = [], []
    for a in range(PEER_TOPK):
        nb = PEER_TOPK // (a + 1)
        halves = [(t2_lo, 0)] + ([(t2_hi, 8)] if nb > 8 else [])
        for t2h, b0 in halves:
            c = top1[a] + t2h
            valid = (row8 + float(b0)) < float(nb)
            cands.append(jnp.where(valid, c, -jnp.inf))
            poss.append(jnp.where(valid, row8 + float(a * PEER_TOPK + b0), 1e9))
    orig = list(cands)
    picked = [jnp.zeros(c.shape, F32) for c in cands]
    for _ in range(PEER_TOPK):
        m = jnp.max(functools.reduce(jnp.maximum, cands), axis=0, keepdims=True)
        sels = [c == m for c in cands]
        if exact:
            hit = [jnp.where(sl, p, 1e9) for sl, p in zip(sels, poss)]
            first = jnp.min(functools.reduce(jnp.minimum, hit), axis=0, keepdims=True)
            sels = [p == first for p in poss]
        cands = [jnp.where(sl, -jnp.inf, c) for sl, c in zip(sels, cands)]
        picked = [jnp.where(sl, 1.0, pk) for sl, pk in zip(sels, picked)]
    m0 = top1[0] + top2[0]
    z = jnp.zeros(lanes, F32)
    count = jnp.zeros(lanes, F32)
    n_sel = []
    k = 0
    for a in range(PEER_TOPK):
        nb = PEER_TOPK // (a + 1)
        n_a = jnp.zeros(lanes, F32)
        for _ in range(2 if nb > 8 else 1):
            n_a = n_a + jnp.sum(picked[k], axis=0, keepdims=True)
            e = jnp.where(picked[k] > 0.5, jnp.exp(orig[k] - m0), 0.0)
            z = z + jnp.sum(e, axis=0, keepdims=True)
            k += 1
        n_sel.append(n_a)
        count = count + n_a
    return n_sel, z, count


def _gelu2(x):
    c0 = 0.7978845608028654
    return x + x * jnp.tanh(x * (c0 + (c0 * 0.044715) * (x * x)))


def _pair_words(x):
    bits = pltpu.bitcast(x.astype(BF16).astype(F32), jnp.uint32)
    return pltpu.bitcast(bits | (bits >> 16), F32)


def _row_tile_bf16(words, il):
    return pltpu.bitcast(jnp.broadcast_to(words[il:il + 1], words.shape), BF16)


def _peer_kernel(h_ref, wq_ref, keys_ref, u_ref, vt_ref, out_ref,
                 xt_s, acc_s, ht_s, at_s, sc_s, rf_s, e1_s, n_s, *, tt, eb):
    e = pl.program_id(1)
    n_chunks = tt // LANE_CHUNK
    rows_per_blk = eb // N_KEYS

    @pl.when(e == 0)
    def _retrieve():
        xt = h_ref[...].astype(F32).T.astype(BF16)
        xt_s[...] = xt
        acc_s[...] = jnp.zeros_like(acc_s)
        qt = _dot(wq_ref[...], xt)
        for hp in range(2 * PEER_HEADS):
            sc_s[hp] = _dot(keys_ref[hp], qt[hp * N_KEYS:(hp + 1) * N_KEYS].astype(BF16))

        def per_head(hc, _):
            h = hc // n_chunks
            c0 = pl.multiple_of((hc % n_chunks) * LANE_CHUNK, LANE_CHUNK)
            lanes = pl.ds(c0, LANE_CHUNK)
            row = lax.broadcasted_iota(jnp.int32, (N_KEYS, LANE_CHUNK), 0).astype(F32)
            row8 = lax.broadcasted_iota(jnp.int32, (8, LANE_CHUNK), 0).astype(F32)
            s1 = sc_s[2 * h, :, lanes]
            s2 = sc_s[2 * h + 1, :, lanes]

            def select(exact):
                rank1, top1, cnt1 = _topk_rows(s1, row, exact)
                rank2, top2, cnt2 = _topk_rows(s2, row, exact)
                n_sel, z, cnt3 = _select_pairs(top1, top2, row8, exact)
                n_dense = jnp.zeros((N_KEYS, LANE_CHUNK), F32)
                for a in range(PEER_TOPK):
                    n_dense = jnp.where(rank1 == float(a), n_sel[a], n_dense)
                n_s[h, :, lanes] = _pair_words(n_dense)
                e1_s[h, :, lanes] = _pair_words(jnp.exp(s1 - top1[0]))
                f = (jnp.exp(s2 - top2[0]) * (0.5 / z)).astype(BF16)
                r2 = rank2.astype(BF16)
                for rg in range(N_KEYS // PACK_ROWS):
                    keys = slice(rg * PACK_ROWS, (rg + 1) * PACK_ROWS)
                    rf_s[hc % n_chunks, rg, 2 * h] = r2[keys]
                    rf_s[hc % n_chunks, rg, 2 * h + 1] = f[keys]
                return jnp.maximum(jnp.maximum(cnt1, cnt2), cnt3)

            most = select(exact=False)

            @pl.when(jnp.max(most) > PEER_TOPK + 0.5)
            def _ties():
                select(exact=True)

            return 0

        lax.fori_loop(0, PEER_HEADS * n_chunks, per_head, 0)

    ht_s[...] = _dot(u_ref[...], xt_s[...])

    def per_chunk(c, _):
        lanes = pl.ds(pl.multiple_of(c * LANE_CHUNK, LANE_CHUNK), LANE_CHUNK)
        grp = pl.ds(pl.multiple_of(e * rows_per_blk, rows_per_blk), rows_per_blk)
        n_grp = [n_s[h, grp, lanes] for h in range(PEER_HEADS)]
        e_grp = [e1_s[h, grp, lanes] for h in range(PEER_HEADS)]
        for il in range(rows_per_blk):
            n16 = [_row_tile_bf16(n_grp[h], il) for h in range(PEER_HEADS)]
            e16 = [_row_tile_bf16(e_grp[h], il) for h in range(PEER_HEADS)]
            for rg in range(N_KEYS // PACK_ROWS):
                w = None
                for h in range(PEER_HEADS):
                    t = jnp.where(rf_s[c, rg, 2 * h] < n16[h], rf_s[c, rg, 2 * h + 1], 0) * e16[h]
                    w = t if w is None else w + t
                rows = pl.ds(il * N_KEYS + rg * PACK_ROWS, PACK_ROWS)
                at_s[rows, lanes] = _gelu2(ht_s[rows, lanes]).astype(BF16) * w
        return 0

    lax.fori_loop(0, n_chunks, per_chunk, 0)
    acc_s[...] += _dot(vt_ref[...], at_s[...])

    @pl.when(e == pl.num_programs(1) - 1)
    def _emit():
        out_ref[...] = acc_s[...].T


def _peer(h_flat, wq_t, keys, u, v_t, tt, eb):
    t, d = h_flat.shape
    n_exp = u.shape[0]
    assert t % tt == 0 and eb == 8 * N_KEYS and n_exp % eb == 0
    scr = pltpu.VMEM
    return pl.pallas_call(
        functools.partial(_peer_kernel, tt=tt, eb=eb),
        grid=(t // tt, n_exp // eb),
        in_specs=[pl.BlockSpec((tt, d), lambda i, e: (i, 0)),
                  pl.BlockSpec(wq_t.shape, lambda i, e: (0, 0)),
                  pl.BlockSpec(keys.shape, lambda i, e: (0, 0, 0)),
                  pl.BlockSpec((eb, d), lambda i, e: (e, 0)),
                  pl.BlockSpec((d, eb), lambda i, e: (0, e))],
        out_specs=pl.BlockSpec((tt, d), lambda i, e: (i, 0)),
        out_shape=jax.ShapeDtypeStruct((t, d), F32),
        scratch_shapes=[scr((d, tt), BF16), scr((d, tt), F32), scr((eb, tt), F32), scr((eb, tt), BF16),
                        scr((2 * PEER_HEADS, N_KEYS, tt), F32),
                        scr((tt // LANE_CHUNK, N_KEYS // PACK_ROWS, 2 * PEER_HEADS, PACK_ROWS, LANE_CHUNK), BF16),
                        scr((PEER_HEADS, N_KEYS, tt), F32), scr((PEER_HEADS, N_KEYS, tt), F32)],
        compiler_params=_cparams(("parallel", "arbitrary")),
        name="peer_dense",
    )(h_flat, wq_t, keys, u, v_t)


def _residual_kernel(x_ref, gate_ref, o_ref, *rest, final):
    x = x_ref[0] + gate_ref[0] * o_ref[0]
    if final:
        g_ref, out_ref = rest
        out_ref[0] = (x * lax.rsqrt(jnp.mean(x * x, axis=-1, keepdims=True) + EPS)) * g_ref[...]
    else:
        (out_ref,) = rest
        out_ref[0] = x


def _residual(x, gate, o, tt, g_final=None):
    n_seq, s, d = x.shape
    final = g_final is not None
    blk = pl.BlockSpec((1, tt, d), lambda b, i: (b, i, 0))
    in_specs = [blk, pl.BlockSpec((1, 1, d), lambda b, i: (b, 0, 0)), blk]
    args = [x, gate.reshape(n_seq, 1, d), o]
    if final:
        in_specs.append(pl.BlockSpec((1, d), lambda b, i: (0, 0)))
        args.append(g_final.reshape(1, d))
    return pl.pallas_call(
        functools.partial(_residual_kernel, final=final),
        grid=(n_seq, s // tt),
        in_specs=in_specs,
        out_specs=blk,
        out_shape=jax.ShapeDtypeStruct(x.shape, F32),
        compiler_params=_cparams(("parallel", "parallel")),
        name="residual",
    )(*args)


def _row_tile(s):
    return 256 if s % 256 == 0 else s


def _peer_tile(t):
    return 512 if t % 512 == 0 else 256


def _pad_lanes(x, n):
    return jnp.pad(x, [(0, 0)] * (x.ndim - 1) + [(0, n - x.shape[-1])])


def _prepare_weights(w_qkv_a, w_o_a, w_kv, w_f, b_f, w_q_b, w_o_b, w_pq, sub_keys, peer_u, peer_v):
    depth = w_pq.shape[0]

    def o_heads(w):
        w = w.reshape(N_HEADS, HEAD_DIM, D_MODEL)
        return jnp.pad(w, ((0, 0), (0, HEAD_LANES - HEAD_DIM), (0, 0))).astype(BF16)

    return dict(
        w_qkv=w_qkv_a[0].astype(BF16),
        w_o_a=o_heads(w_o_a[0]),
        w_kvf=jnp.concatenate([w_kv, _pad_lanes(w_f, HEAD_LANES)], axis=1).astype(BF16),
        b_f=_pad_lanes(b_f, HEAD_LANES),
        w_q_b=w_q_b[0].astype(BF16),
        w_o_b=o_heads(w_o_b[0]),
        w_pq_t=[w_pq[l].T.astype(BF16) for l in range(depth)],
        keys=[jnp.swapaxes(sub_keys[l], 0, 1).reshape(2 * PEER_HEADS, N_KEYS, -1).astype(BF16)
              for l in range(depth)],
        u=[peer_u[l].astype(BF16) for l in range(depth)],
        v_t=[peer_v[l].T.astype(BF16) for l in range(depth)],
    )


def _peer_layer(x, g, shift, scale, gate, wts, l, g_final=None):
    n_seq, s, d = x.shape
    tt = _row_tile(s)
    h = _mod_norm_call(x, g, shift, scale, tt).reshape(n_seq * s, d)
    o = _peer(h, wts["w_pq_t"][l], wts["keys"][l], wts["u"][l], wts["v_t"][l],
              tt=_peer_tile(n_seq * s), eb=1024)
    return _residual(x, gate, o.reshape(n_seq, s, d), tt, g_final=g_final)


def _trunk(x, mods0, mods1, modkv, caches, wts, g_norm, g_kv, g_final, rel_bias):
    n_seq, s, d = x.shape
    tt = _row_tile(s)
    sh_m, sc_m, gt_m, sh_c, sc_c, gt_c = jnp.split(mods0, 6, axis=-1)

    q, k, v = _norm_proj(x, g_norm[0, 0], sh_m, sc_m, wts["w_qkv"], (d, d, d), tt)
    qh = _to_heads(q, tt, scale=HEAD_DIM ** -0.5)
    if caches is None:
        kh, vh = _to_heads(k, tt), _to_heads(v, tt)
        bias = _relbias_table(rel_bias, BAND, 2 * BAND, split=768, band_mask=True)
        oh = _band_attend_prompt(qh, kh, vh, bias)
        keep = min(BAND, s)
        new_ak, new_av = k[:, s - keep:], v[:, s - keep:]
    else:
        cache_a_k, cache_a_v = caches[0], caches[1]
        win = cache_a_k.shape[2]
        k_all = jnp.concatenate([cache_a_k[0].reshape(n_seq, win, d), k], axis=1)
        v_all = jnp.concatenate([cache_a_v[0].reshape(n_seq, win, d), v], axis=1)
        kh, vh = _to_heads(k_all, win + s), _to_heads(v_all, win + s)
        lanes = -(-(win + s) // 128) * 128
        bias = _relbias_table(rel_bias, s, win + s, split=(win + s + lanes - s) // 2, band_mask=False)
        oh = _band_attend_step(qh, kh, vh, bias)
        new_ak, new_av = k, v
    x = _oproj_residual(oh, wts["w_o_a"], x, gt_m, tt)
    x = _peer_layer(x, g_norm[0, 1], sh_c, sc_c, gt_c, wts, 0)

    sh_m, sc_m, gt_m, sh_c, sc_c, gt_c = jnp.split(mods1, 6, axis=-1)
    sh_kv, sc_kv = jnp.split(modkv, 2, axis=-1)
    bk, bv, lf = _norm_proj(x, g_kv, sh_kv, sc_kv, wts["w_kvf"], (d, d, HEAD_LANES), tt, bias_last=wts["b_f"])
    (q,) = _norm_proj(x, g_norm[1, 0], sh_m, sc_m, wts["w_q_b"], (d,), tt)
    if caches is None:
        past = 0
        k_all, v_all, lf_all = bk, bv, lf
        tq = tk = 512 if s % 512 == 0 else s
        bs = 256 if s % 256 == 0 else s
    else:
        cache_b_k, cache_b_v, cache_b_logf = caches[2], caches[3], caches[4]
        past = cache_b_k.shape[1]
        k_all = jnp.concatenate([cache_b_k.reshape(n_seq, past, d), bk], axis=1)
        v_all = jnp.concatenate([cache_b_v.reshape(n_seq, past, d), bv], axis=1)
        lf_all = jnp.concatenate([_pad_lanes(cache_b_logf.astype(F32), HEAD_LANES), lf], axis=1)
        tq, tk = s, 512
        bs = s
    cum = _cumsum_seq(lf_all, bs)
    tk_rows = _row_tile(past + s) if (past + s) % 256 == 0 else s
    qh = _to_heads(q, tt, scale=HEAD_DIM ** -0.5, mode="q_aug", cum=cum[:, past:])
    kh = _to_heads(k_all, tk_rows, mode="k_aug", cum=cum)
    vh = _to_heads(v_all, tk_rows)
    oh = _fox_attend(qh, kh, vh, tq, tk, past)
    x = _oproj_residual(oh, wts["w_o_b"], x, gt_m, tt)
    y = _peer_layer(x, g_norm[1, 1], sh_c, sc_c, gt_c, wts, 1, g_final=g_final)

    hshape = (n_seq, -1, N_HEADS, HEAD_DIM)
    return (y, new_ak.reshape(hshape)[None], new_av.reshape(hshape)[None],
            bk.reshape(hshape), bv.reshape(hshape), lf[:, :, :N_HEADS])


def kernel(x_prompt, x_sample, c_prompt, c_sample, cache_a_k, cache_a_v, cache_b_k, cache_b_v, cache_b_logf, w_ada, b_ada, g_norm, w_qkv_a, rel_bias_a, w_o_a, w_ada_kv, b_ada_kv, g_kv, w_kv, w_f, b_f, w_q_b, w_o_b, w_pq, sub_keys, peer_u, peer_v, g_final):
    n_p, n_s = c_prompt.shape[0], c_sample.shape[0]
    n_c = -(-(n_p + n_s) // 16) * 16
    c_all = jnp.pad(jnp.concatenate([c_prompt, c_sample], axis=0), ((0, n_c - n_p - n_s), (0, 0)))
    mods0 = _cond_matmul(c_all, w_ada[0], b_ada[0])
    mods1 = _cond_matmul(c_all, w_ada[1], b_ada[1])
    modkv = _cond_matmul(c_all, w_ada_kv, b_ada_kv)
    wts = _prepare_weights(w_qkv_a, w_o_a, w_kv, w_f, b_f, w_q_b, w_o_b, w_pq, sub_keys, peer_u, peer_v)

    out_p = _trunk(x_prompt, mods0[:n_p], mods1[:n_p], modkv[:n_p], None,
                   wts, g_norm, g_kv, g_final, rel_bias_a[0])
    out_s = _trunk(x_sample, mods0[n_p:n_p + n_s], mods1[n_p:n_p + n_s], modkv[n_p:n_p + n_s],
                   (cache_a_k, cache_a_v, cache_b_k, cache_b_v, cache_b_logf),
                   wts, g_norm, g_kv, g_final, rel_bias_a[0])
    return (out_p[0], out_s[0]) + out_p[1:] + out_s[1:]
```

```python
import functools

import jax
import jax.numpy as jnp
from jax import lax
from jax.experimental import pallas as pl
from jax.experimental.pallas import tpu as pltpu

F32 = jnp.float32
BF16 = jnp.bfloat16

D_MODEL = 1024
HEAD_DIM = 64
N_HEADS = D_MODEL // HEAD_DIM
HEAD_LANES = 128
HEADS_PER_STEP = 2
CHUNK = 64
BAND_CHUNKS = 8
BAND = BAND_CHUNKS * CHUNK
REL_CLIP = 128
PEER_HEADS = 8
N_KEYS = 128
PEER_TOPK = 16
EPS = 1e-6
NEG_INF = -1e30

VMEM_LIMIT = 56 * 1024 * 1024


def _cparams(sem, vmem=VMEM_LIMIT):
    return pltpu.CompilerParams(dimension_semantics=sem, vmem_limit_bytes=vmem)


def _dot(a, b):
    return jnp.dot(a, b, preferred_element_type=F32)


def _dot_nt(a, b):
    return lax.dot_general(a, b, (((1,), (1,)), ((), ())), preferred_element_type=F32)


def _split3(x):
    hi = x.astype(BF16)
    r = x - hi.astype(F32)
    mid = r.astype(BF16)
    lo = (r - mid.astype(F32)).astype(BF16)
    return hi, mid, lo


def _cond_kernel(c_ref, w_ref, b_ref, o_ref):
    ch, cm, cl = _split3(c_ref[...])
    wh, wm, wl = _split3(w_ref[...])
    acc = _dot(ch, wh) + (_dot(ch, wm) + _dot(cm, wh)) + (_dot(ch, wl) + _dot(cl, wh) + _dot(cm, wm))
    o_ref[...] = acc + b_ref[...]


def _cond_matmul(c, w, b):
    m, d = c.shape
    n = w.shape[1]
    bn = 1024
    return pl.pallas_call(
        _cond_kernel,
        grid=(n // bn,),
        in_specs=[pl.BlockSpec((m, d), lambda j: (0, 0)),
                  pl.BlockSpec((d, bn), lambda j: (0, j)),
                  pl.BlockSpec((1, bn), lambda j: (0, j))],
        out_specs=pl.BlockSpec((m, bn), lambda j: (0, j)),
        out_shape=jax.ShapeDtypeStruct((m, n), F32),
        compiler_params=_cparams(("arbitrary",)),
        name="cond_matmul",
    )(c, w, b.reshape(1, n))


def _mod_norm(x, g, shift, scale):
    y = x * lax.rsqrt(jnp.mean(x * x, axis=-1, keepdims=True) + EPS)
    return (y * g) * (1.0 + scale) + shift


def _log_sigmoid(x):
    return jnp.minimum(x, 0.0) - jnp.log1p(jnp.exp(-jnp.abs(x)))


def _normproj_kernel(x_ref, g_ref, sh_ref, sc_ref, w_ref, *rest, splits, logsig_last):
    if logsig_last:
        bl_ref, o_refs = rest[0], rest[1:]
    else:
        o_refs = rest
    h = _mod_norm(x_ref[0], g_ref[...], sh_ref[0], sc_ref[0])
    out = _dot(h.astype(BF16), w_ref[...])
    off = 0
    for k, (o_ref, n) in enumerate(zip(o_refs, splits)):
        piece = out[:, off:off + n]
        if logsig_last and k == len(splits) - 1:
            piece = _log_sigmoid(piece + bl_ref[...])
        o_ref[0] = piece
        off += n


def _norm_proj(x, g, shift, scale, w, splits, tt, bias_last=None):
    n_seq, s, d = x.shape
    n = w.shape[1]
    logsig_last = bias_last is not None
    vec = pl.BlockSpec((1, 1, d), lambda b, i: (b, 0, 0))
    in_specs = [pl.BlockSpec((1, tt, d), lambda b, i: (b, i, 0)),
                pl.BlockSpec((1, d), lambda b, i: (0, 0)), vec, vec,
                pl.BlockSpec((d, n), lambda b, i: (0, 0))]
    args = [x, g.reshape(1, d), shift.reshape(n_seq, 1, d), scale.reshape(n_seq, 1, d), w]
    if logsig_last:
        in_specs.append(pl.BlockSpec((1, splits[-1]), lambda b, i: (0, 0)))
        args.append(bias_last.reshape(1, splits[-1]))
    return pl.pallas_call(
        functools.partial(_normproj_kernel, splits=tuple(splits), logsig_last=logsig_last),
        grid=(n_seq, s // tt),
        in_specs=in_specs,
        out_specs=[pl.BlockSpec((1, tt, k), lambda b, i: (b, i, 0)) for k in splits],
        out_shape=[jax.ShapeDtypeStruct((n_seq, s, k), F32) for k in splits],
        compiler_params=_cparams(("parallel", "parallel")),
        name="norm_proj",
    )(*args)


def _write_heads(x, o_ref, scale, mode, cum):
    tt = x.shape[0]
    lane = lax.broadcasted_iota(jnp.int32, (tt, HEAD_LANES), 1)
    for pair in range(N_HEADS // 2):
        col = x[:, pair * HEAD_LANES:(pair + 1) * HEAD_LANES]
        for par in range(2):
            h = 2 * pair + par
            c = col if par == 0 else pltpu.roll(col, HEAD_DIM, 1)
            if scale != 1.0:
                c = c * scale
            if mode == "plain":
                aug = jnp.zeros_like(c)
            else:
                cc = cum[:, h:h + 1]
                c1 = cc.astype(BF16).astype(F32)
                r = cc - c1
                c2 = r.astype(BF16).astype(F32)
                c3 = r - c2
                one = jnp.ones_like(c)
                if mode == "q_aug":
                    aug = jnp.where(lane == 64, c1, jnp.where(lane == 65, c2, jnp.where(lane == 66, c3, one)))
                else:
                    aug = jnp.where(lane == 67, -c1, jnp.where(lane == 68, -c2, jnp.where(lane == 69, -c3, one)))
                aug = jnp.where(lane < 70, aug, 0.0)
            o_ref[0, h] = jnp.where(lane < HEAD_DIM, c, aug).astype(BF16)


def _toheads_kernel(x_ref, *rest, scale, mode):
    if mode == "plain":
        (o_ref,) = rest
        cum = None
    else:
        cum_ref, o_ref = rest
        cum = cum_ref[0]
    _write_heads(x_ref[0], o_ref, scale, mode, cum)


def _projheads_kernel(*refs, n_mod, use_cum, plan):
    x_ref = refs[0]
    mods = [refs[1 + 4 * i:5 + 4 * i] for i in range(n_mod)]
    pos = 1 + 4 * n_mod
    cum = refs[pos][0] if use_cum else None
    out_refs = refs[pos + use_cum:]
    x = x_ref[0]
    y = x * lax.rsqrt(jnp.mean(x * x, axis=-1, keepdims=True) + EPS)
    outs = []
    for g_ref, sh_ref, sc_ref, w_ref in mods:
        h = (y * g_ref[...]) * (1.0 + sc_ref[0]) + sh_ref[0]
        outs.append(_dot(h.astype(BF16), w_ref[...]))
    for (m, off, kind), o_ref in zip(plan, out_refs):
        blk = outs[m][:, off:off + D_MODEL]
        if kind == "nat":
            o_ref[0] = blk
        else:
            _write_heads(blk, o_ref, kind[0], kind[1], cum)


def _proj_heads(x, mods, plan, tt, cum=None):
    n_seq, s, d = x.shape
    vec = pl.BlockSpec((1, 1, d), lambda b, i: (b, 0, 0))
    in_specs = [pl.BlockSpec((1, tt, d), lambda b, i: (b, i, 0))]
    args = [x]
    for g, shift, scale, w in mods:
        in_specs += [pl.BlockSpec((1, d), lambda b, i: (0, 0)), vec, vec,
                     pl.BlockSpec(w.shape, lambda b, i: (0, 0))]
        args += [g.reshape(1, d), shift.reshape(n_seq, 1, d), scale.reshape(n_seq, 1, d), w]
    if cum is not None:
        in_specs.append(pl.BlockSpec((1, tt, HEAD_LANES), lambda b, i: (b, i, 0)))
        args.append(cum)
    out_specs, out_shape = [], []
    for _, _, kind in plan:
        if kind == "nat":
            out_specs.append(pl.BlockSpec((1, tt, d), lambda b, i: (b, i, 0)))
            out_shape.append(jax.ShapeDtypeStruct((n_seq, s, d), F32))
        else:
            out_specs.append(pl.BlockSpec((1, N_HEADS, tt, HEAD_LANES), lambda b, i: (b, 0, i, 0)))
            out_shape.append(jax.ShapeDtypeStruct((n_seq, N_HEADS, s, HEAD_LANES), BF16))
    return pl.pallas_call(
        functools.partial(_projheads_kernel, n_mod=len(mods), use_cum=cum is not None, plan=tuple(plan)),
        grid=(n_seq, s // tt),
        in_specs=in_specs,
        out_specs=out_specs,
        out_shape=out_shape,
        compiler_params=_cparams(("parallel", "parallel")),
        name="proj_heads",
    )(*args)


def _to_heads(x, tt, scale=1.0, mode="plain", cum=None):
    n_seq, s, d = x.shape
    in_specs = [pl.BlockSpec((1, tt, d), lambda b, i: (b, i, 0))]
    args = [x]
    if mode != "plain":
        in_specs.append(pl.BlockSpec((1, tt, HEAD_LANES), lambda b, i: (b, i, 0)))
        args.append(cum)
    return pl.pallas_call(
        functools.partial(_toheads_kernel, scale=scale, mode=mode),
        grid=(n_seq, s // tt),
        in_specs=in_specs,
        out_specs=pl.BlockSpec((1, N_HEADS, tt, HEAD_LANES), lambda b, i: (b, 0, i, 0)),
        out_shape=jax.ShapeDtypeStruct((n_seq, N_HEADS, s, HEAD_LANES), BF16),
        compiler_params=_cparams(("parallel", "parallel")),
        name="to_heads",
    )(*args)


def _relbias_kernel(rb_ref, o_ref, *, n_q, n_k, split, band_mask):
    h = pl.program_id(0)
    lanes = o_ref.shape[2]
    idx = lax.broadcasted_iota(jnp.int32, (8, lanes), 1)
    dd = jnp.where(idx > split, BAND + (lanes - idx), BAND - idx)
    v = jnp.clip(dd, -REL_CLIP, REL_CLIP) + REL_CLIP

    def fill(k, t):
        return jnp.where(v == k, rb_ref[h, k], t)

    t = lax.fori_loop(0, 2 * REL_CLIP + 1, fill, jnp.zeros((8, lanes), F32))
    x = jnp.broadcast_to(t[0:1, :], (n_q, lanes))
    row = lax.broadcasted_iota(jnp.int32, (n_q, lanes), 0)
    bit = 1
    while bit < n_q:
        x = jnp.where((row & bit) != 0, pltpu.roll(x, bit, 1), x)
        bit *= 2
    if band_mask:
        col = lax.broadcasted_iota(jnp.int32, (n_q, lanes), 1)
        dc = BAND_CHUNKS + (row >> 6) - (col >> 6)
        x = jnp.where((dc >= 0) & (dc <= BAND_CHUNKS), x, NEG_INF)
    o_ref[0] = x


def _relbias_table(rel_bias, n_q, n_k, split, band_mask):
    lanes = -(-n_k // 128) * 128
    return pl.pallas_call(
        functools.partial(_relbias_kernel, n_q=n_q, n_k=n_k, split=split, band_mask=band_mask),
        grid=(N_HEADS,),
        in_specs=[pl.BlockSpec(memory_space=pltpu.SMEM)],
        out_specs=pl.BlockSpec((1, n_q, lanes), lambda h: (h, 0, 0)),
        out_shape=jax.ShapeDtypeStruct((N_HEADS, n_q, lanes), F32),
        compiler_params=_cparams(("arbitrary",)),
        name="relbias_table",
    )(rel_bias)


def _band_kernel(*refs, n_seg, first_seg_is_pad_at_zero):
    q_ref = refs[0]
    k_refs = refs[1:1 + n_seg]
    v_refs = refs[1 + n_seg:1 + 2 * n_seg]
    bias_ref = refs[1 + 2 * n_seg]
    o_ref = refs[2 + 2 * n_seg]
    for j in range(HEADS_PER_STEP):
        q = q_ref[0, j]
        scores = []
        off = 0
        for i in range(n_seg):
            k = k_refs[i][0, j]
            nk = k.shape[0]
            s = _dot_nt(q, k) + bias_ref[j, :, off:off + nk]
            if i == 0 and first_seg_is_pad_at_zero:
                s = jnp.where(pl.program_id(2) > 0, s, NEG_INF)
            scores.append(s)
            off += nk
        m = functools.reduce(jnp.maximum, [jnp.max(s, axis=-1, keepdims=True) for s in scores])
        ps = [jnp.exp(s - m) for s in scores]
        l = functools.reduce(jnp.add, [jnp.sum(p, axis=-1, keepdims=True) for p in ps])
        acc = functools.reduce(jnp.add, [_dot(p.astype(BF16), v_ref[0, j]) for p, v_ref in zip(ps, v_refs)])
        o_ref[0, j] = (acc / l).astype(BF16)


def _band_attend_prompt(qh, kh, vh, bias):
    n_seq, nh, s, hl = qh.shape
    tg = BAND
    blk = (1, HEADS_PER_STEP, tg, hl)
    cur = lambda h, b, g: (b, h, g, 0)
    prev = lambda h, b, g: (b, h, jnp.maximum(g - 1, 0), 0)
    return pl.pallas_call(
        functools.partial(_band_kernel, n_seg=2, first_seg_is_pad_at_zero=True),
        grid=(nh // HEADS_PER_STEP, n_seq, s // tg),
        in_specs=[pl.BlockSpec(blk, cur),
                  pl.BlockSpec(blk, prev), pl.BlockSpec(blk, cur),
                  pl.BlockSpec(blk, prev), pl.BlockSpec(blk, cur),
                  pl.BlockSpec((HEADS_PER_STEP, tg, 2 * tg), lambda h, b, g: (h, 0, 0))],
        out_specs=pl.BlockSpec(blk, cur),
        out_shape=jax.ShapeDtypeStruct(qh.shape, BF16),
        compiler_params=_cparams(("arbitrary", "arbitrary", "arbitrary")),
        name="band_attend_prompt",
    )(qh, kh, kh, vh, vh, bias)


def _band_attend_step(qh, kh, vh, bias):
    n_seq, nh, sq, hl = qh.shape
    sk = kh.shape[2]
    hps = HEADS_PER_STEP
    return pl.pallas_call(
        functools.partial(_band_kernel, n_seg=1, first_seg_is_pad_at_zero=False),
        grid=(nh // hps, n_seq, 1),
        in_specs=[pl.BlockSpec((1, hps, sq, hl), lambda h, b, g: (b, h, 0, 0)),
                  pl.BlockSpec((1, hps, sk, hl), lambda h, b, g: (b, h, 0, 0)),
                  pl.BlockSpec((1, hps, sk, hl), lambda h, b, g: (b, h, 0, 0)),
                  pl.BlockSpec((hps, sq, bias.shape[2]), lambda h, b, g: (h, 0, 0))],
        out_specs=pl.BlockSpec((1, hps, sq, hl), lambda h, b, g: (b, h, 0, 0)),
        out_shape=jax.ShapeDtypeStruct(qh.shape, BF16),
        compiler_params=_cparams(("arbitrary", "arbitrary", "arbitrary")),
        name="band_attend_step",
    )(qh, kh, vh, bias)


def _fox_kernel(q_ref, k_ref, v_ref, o_ref, *, tq, tk, past):
    qi = pl.program_id(2)
    heads = range(HEADS_PER_STEP)
    qs = [q_ref[0, j] for j in heads]

    def update(carry, s, v):
        m, l, acc = carry
        m_new = jnp.maximum(m, jnp.max(s, axis=-1, keepdims=True))
        alpha = jnp.exp(m - m_new)
        p = jnp.exp(s - m_new)
        l = alpha * l + jnp.sum(p, axis=-1, keepdims=True)
        acc = alpha * acc + _dot(p.astype(BF16), v)
        return m_new, l, acc

    def full_tile(t, carries):
        rows = pl.ds(pl.multiple_of(t * tk, tk), tk)
        return tuple(update(carries[j], _dot_nt(qs[j], k_ref[0, j, rows, :]), v_ref[0, j, rows, :])
                     for j in heads)

    init = tuple((jnp.full((tq, 1), NEG_INF, F32), jnp.zeros((tq, 1), F32), jnp.zeros((tq, HEAD_LANES), F32))
                 for _ in heads)
    q_start = past + qi * tq
    carries = lax.fori_loop(0, q_start // tk, full_tile, init)
    rows = pl.ds(pl.multiple_of(q_start, tq), tq)
    row = lax.broadcasted_iota(jnp.int32, (tq, tq), 0)
    col = lax.broadcasted_iota(jnp.int32, (tq, tq), 1)
    for j in heads:
        s = jnp.where(col <= row, _dot_nt(qs[j], k_ref[0, j, rows, :]), NEG_INF)
        m, l, acc = update(carries[j], s, v_ref[0, j, rows, :])
        o_ref[0, j] = (acc / l).astype(BF16)


def _fox_attend(qh, kh, vh, tq, tk, past):
    n_seq, nh, sq, hl = qh.shape
    sk = kh.shape[2]
    hps = HEADS_PER_STEP
    return pl.pallas_call(
        functools.partial(_fox_kernel, tq=tq, tk=tk, past=past),
        grid=(n_seq, nh // hps, sq // tq),
        in_specs=[pl.BlockSpec((1, hps, tq, hl), lambda b, h, i: (b, h, i, 0)),
                  pl.BlockSpec((1, hps, sk, hl), lambda b, h, i: (b, h, 0, 0)),
                  pl.BlockSpec((1, hps, sk, hl), lambda b, h, i: (b, h, 0, 0))],
        out_specs=pl.BlockSpec((1, hps, tq, hl), lambda b, h, i: (b, h, i, 0)),
        out_shape=jax.ShapeDtypeStruct(qh.shape, BF16),
        compiler_params=_cparams(("arbitrary", "arbitrary", "arbitrary")),
        name="fox_attend",
    )(qh, kh, vh)


def _cumsum_kernel(x_ref, o_ref, *, bs):
    n_blk = x_ref.shape[1] // bs
    r = lax.broadcasted_iota(jnp.int32, (bs, bs), 0)
    c = lax.broadcasted_iota(jnp.int32, (bs, bs), 1)
    tri = jnp.where(c <= r, 1.0, 0.0).astype(BF16)

    def body(i, carry):
        start = pl.multiple_of(i * bs, bs)
        hi, mid, lo = _split3(x_ref[0, pl.ds(start, bs), :])
        out = (_dot(tri, hi) + _dot(tri, mid) + _dot(tri, lo)) + carry
        o_ref[0, pl.ds(start, bs), :] = out
        return out[bs - 1:bs, :]

    lax.fori_loop(0, n_blk, body, jnp.zeros((1, x_ref.shape[2]), F32))


def _cumsum_seq(x, bs):
    n_seq, s, w = x.shape
    return pl.pallas_call(
        functools.partial(_cumsum_kernel, bs=bs),
        grid=(n_seq,),
        in_specs=[pl.BlockSpec((1, s, w), lambda b: (b, 0, 0))],
        out_specs=pl.BlockSpec((1, s, w), lambda b: (b, 0, 0)),
        out_shape=jax.ShapeDtypeStruct(x.shape, F32),
        compiler_params=_cparams(("parallel",)),
        name="cumsum_seq",
    )(x)


def _oproj_kernel(o_ref, w_ref, x_ref, gate_ref, g_ref, sh_ref, sc_ref, out_ref, h_ref):
    acc = _dot(o_ref[0, 0], w_ref[0])
    for h in range(1, N_HEADS):
        acc = acc + _dot(o_ref[0, h], w_ref[h])
    x = x_ref[0] + gate_ref[0] * acc
    out_ref[0] = x
    h_ref[0] = _mod_norm(x, g_ref[...], sh_ref[0], sc_ref[0]).astype(BF16)


def _oproj_residual(oh, w_heads, x, gate, g, shift, scale, tt):
    n_seq, s, d = x.shape
    vec = pl.BlockSpec((1, 1, d), lambda b, i: (b, 0, 0))
    blk = pl.BlockSpec((1, tt, d), lambda b, i: (b, i, 0))
    return pl.pallas_call(
        _oproj_kernel,
        grid=(n_seq, s // tt),
        in_specs=[pl.BlockSpec((1, N_HEADS, tt, HEAD_LANES), lambda b, i: (b, 0, i, 0)),
                  pl.BlockSpec((N_HEADS, HEAD_LANES, d), lambda b, i: (0, 0, 0)),
                  blk, vec, pl.BlockSpec((1, d), lambda b, i: (0, 0)), vec, vec],
        out_specs=[blk, blk],
        out_shape=[jax.ShapeDtypeStruct(x.shape, F32), jax.ShapeDtypeStruct(x.shape, BF16)],
        compiler_params=_cparams(("parallel", "parallel")),
        name="oproj_residual",
    )(oh, w_heads, x, gate.reshape(n_seq, 1, d), g.reshape(1, d),
      shift.reshape(n_seq, 1, d), scale.reshape(n_seq, 1, d))


LANE_CHUNK = 128
SUB_ROWS = 8
PACK_ROWS = 16


def _topk_rows(s, row, exact):
    rank = jnp.full(s.shape, float(PEER_TOPK), F32)
    tops = []
    for r in range(PEER_TOPK):
        m = jnp.max(s, axis=0, keepdims=True)
        sel = s == m
        if exact:
            first = jnp.min(jnp.where(sel, row, float(N_KEYS)), axis=0, keepdims=True)
            sel = row == first
        rank = jnp.where(sel, float(r), rank)
        s = jnp.where(sel, -jnp.inf, s)
        tops.append(m)
    count = jnp.sum(jnp.where(rank < float(PEER_TOPK), 1.0, 0.0), axis=0, keepdims=True)
    return rank, tops, count


def _stack16(rows, row16):
    out = jnp.zeros(row16.shape, F32)
    for b, v in enumerate(rows):
        out = jnp.where(row16 == float(b), v, out)
    return out


def _select_pairs(top1, top2, row8, exact):
    lanes = top1[0].shape
    row16 = jnp.concatenate([row8, row8 + 8.0], axis=0)
    t2 = _stack16(top2, row16)
    t2_lo, t2_hi = t2[0:8], t2[8:16]
    cands, poss = [], []
    for a in range(PEER_TOPK):
        nb = PEER_TOPK // (a + 1)
        halves = [(t2_lo, 0)] + ([(t2_hi, 8)] if nb > 8 else [])
        for t2h, b0 in halves:
            c = top1[a] + t2h
            valid = (row8 + float(b0)) < float(nb)
            cands.append(jnp.where(valid, c, -jnp.inf))
            poss.append(jnp.where(valid, row8 + float(a * PEER_TOPK + b0), 1e9))
    orig = list(cands)
    picked = [jnp.zeros(c.shape, F32) for c in cands]
    for _ in range(PEER_TOPK):
        m = jnp.max(functools.reduce(jnp.maximum, cands), axis=0, keepdims=True)
        sels = [c == m for c in cands]
        if exact:
            hit = [jnp.where(sl, p, 1e9) for sl, p in zip(sels, poss)]
            first = jnp.min(functools.reduce(jnp.minimum, hit), axis=0, keepdims=True)
            sels = [p == first for p in poss]
        cands = [jnp.where(sl, -jnp.inf, c) for sl, c in zip(sels, cands)]
        picked = [jnp.where(sl, 1.0, pk) for sl, pk in zip(sels, picked)]
    m0 = top1[0] + top2[0]
    z = jnp.zeros(lanes, F32)
    count = jnp.zeros(lanes, F32)
    n_sel = []
    k = 0
    for a in range(PEER_TOPK):
        nb = PEER_TOPK // (a + 1)
        n_a = jnp.zeros(lanes, F32)
        for _ in range(2 if nb > 8 else 1):
            n_a = n_a + jnp.sum(picked[k], axis=0, keepdims=True)
            e = jnp.where(picked[k] > 0.5, jnp.exp(orig[k] - m0), 0.0)
            z = z + jnp.sum(e, axis=0, keepdims=True)
            k += 1
        n_sel.append(n_a)
        count = count + n_a
    return n_sel, z, count


def _gelu2(x):
    c0 = 0.7978845608028654
    return x + x * jnp.tanh(x * (c0 + (c0 * 0.044715) * (x * x)))


def _peer_kernel(h_ref, wq_ref, keys_ref, u_ref, vt_ref, out_ref,
                 xt_s, acc_s, ht_s, at_s, sc_s, rf_s, e1_s, n_s, *, tt, eb):
    e = pl.program_id(1)
    n_chunks = tt // LANE_CHUNK
    rows_per_blk = eb // N_KEYS

    @pl.when(e == 0)
    def _retrieve():
        xt = h_ref[...].astype(F32).T.astype(BF16)
        xt_s[...] = xt
        acc_s[...] = jnp.zeros_like(acc_s)
        qt = _dot(wq_ref[...], xt)
        for hp in range(2 * PEER_HEADS):
            sc_s[hp] = _dot(keys_ref[hp], qt[hp * N_KEYS:(hp + 1) * N_KEYS].astype(BF16))

        def per_head(hc, _):
            h = hc // n_chunks
            c0 = pl.multiple_of((hc % n_chunks) * LANE_CHUNK, LANE_CHUNK)
            lanes = pl.ds(c0, LANE_CHUNK)
            row = lax.broadcasted_iota(jnp.int32, (N_KEYS, LANE_CHUNK), 0).astype(F32)
            row8 = lax.broadcasted_iota(jnp.int32, (8, LANE_CHUNK), 0).astype(F32)
            s1 = sc_s[2 * h, :, lanes]
            s2 = sc_s[2 * h + 1, :, lanes]

            def select(exact):
                rank1, top1, cnt1 = _topk_rows(s1, row, exact)
                rank2, top2, cnt2 = _topk_rows(s2, row, exact)
                n_sel, z, cnt3 = _select_pairs(top1, top2, row8, exact)
                n_dense = jnp.zeros((N_KEYS, LANE_CHUNK), F32)
                for a in range(PEER_TOPK):
                    n_dense = jnp.where(rank1 == float(a), n_sel[a], n_dense)
                n_s[h, :, lanes] = n_dense
                e1_s[h, :, lanes] = jnp.exp(s1 - top1[0])
                f = jnp.exp(s2 - top2[0]) * (0.5 / z)
                for rt in range(N_KEYS // SUB_ROWS):
                    keys = slice(rt * SUB_ROWS, (rt + 1) * SUB_ROWS)
                    rf_s[hc % n_chunks, rt, 2 * h] = rank2[keys]
                    rf_s[hc % n_chunks, rt, 2 * h + 1] = f[keys]
                return jnp.maximum(jnp.maximum(cnt1, cnt2), cnt3)

            most = select(exact=False)

            @pl.when(jnp.max(most) > PEER_TOPK + 0.5)
            def _ties():
                select(exact=True)

            return 0

        lax.fori_loop(0, PEER_HEADS * n_chunks, per_head, 0)

    ht_s[...] = _dot(u_ref[...], xt_s[...])

    def per_chunk(c, _):
        lanes = pl.ds(pl.multiple_of(c * LANE_CHUNK, LANE_CHUNK), LANE_CHUNK)
        grp = pl.ds(pl.multiple_of(e * rows_per_blk, rows_per_blk), rows_per_blk)
        n_grp = [n_s[h, grp, lanes] for h in range(PEER_HEADS)]
        e_grp = [e1_s[h, grp, lanes] for h in range(PEER_HEADS)]
        tile = (SUB_ROWS, LANE_CHUNK)
        for il in range(rows_per_blk):
            n8 = [jnp.broadcast_to(n_grp[h][il:il + 1], tile) for h in range(PEER_HEADS)]
            e8 = [jnp.broadcast_to(e_grp[h][il:il + 1], tile) for h in range(PEER_HEADS)]
            for rp in range(N_KEYS // PACK_ROWS):
                halves = []
                for rt in (2 * rp, 2 * rp + 1):
                    w = None
                    for h in range(PEER_HEADS):
                        t = jnp.where(rf_s[c, rt, 2 * h] < n8[h], rf_s[c, rt, 2 * h + 1], 0.0) * e8[h]
                        w = t if w is None else w + t
                    rows = pl.ds(il * N_KEYS + rt * SUB_ROWS, SUB_ROWS)
                    halves.append(_gelu2(ht_s[rows, lanes]) * w)
                rows = pl.ds(il * N_KEYS + rp * PACK_ROWS, PACK_ROWS)
                at_s[rows, lanes] = jnp.concatenate(halves, axis=0).astype(BF16)
        return 0

    lax.fori_loop(0, n_chunks, per_chunk, 0)
    acc_s[...] += _dot(vt_ref[...], at_s[...])

    @pl.when(e == pl.num_programs(1) - 1)
    def _emit():
        out_ref[...] = acc_s[...].T


def _peer(h_flat, wq_t, keys, u, v_t, tt, eb):
    t, d = h_flat.shape
    n_exp = u.shape[0]
    assert t % tt == 0 and eb == 8 * N_KEYS and n_exp % eb == 0
    scr = pltpu.VMEM
    return pl.pallas_call(
        functools.partial(_peer_kernel, tt=tt, eb=eb),
        grid=(t // tt, n_exp // eb),
        in_specs=[pl.BlockSpec((tt, d), lambda i, e: (i, 0)),
                  pl.BlockSpec(wq_t.shape, lambda i, e: (0, 0)),
                  pl.BlockSpec(keys.shape, lambda i, e: (0, 0, 0)),
                  pl.BlockSpec((eb, d), lambda i, e: (e, 0)),
                  pl.BlockSpec((d, eb), lambda i, e: (0, e))],
        out_specs=pl.BlockSpec((tt, d), lambda i, e: (i, 0)),
        out_shape=jax.ShapeDtypeStruct((t, d), F32),
        scratch_shapes=[scr((d, tt), BF16), scr((d, tt), F32), scr((eb, tt), F32), scr((eb, tt), BF16),
                        scr((2 * PEER_HEADS, N_KEYS, tt), F32),
                        scr((tt // LANE_CHUNK, N_KEYS // SUB_ROWS, 2 * PEER_HEADS, SUB_ROWS, LANE_CHUNK), F32),
                        scr((PEER_HEADS, N_KEYS, tt), F32), scr((PEER_HEADS, N_KEYS, tt), F32)],
        compiler_params=_cparams(("parallel", "arbitrary")),
        name="peer_dense",
    )(h_flat, wq_t, keys, u, v_t)


def _residual_kernel(x_ref, gate_ref, o_ref, *rest, final):
    x = x_ref[0] + gate_ref[0] * o_ref[0]
    if final:
        g_ref, out_ref = rest
        out_ref[0] = (x * lax.rsqrt(jnp.mean(x * x, axis=-1, keepdims=True) + EPS)) * g_ref[...]
    else:
        (out_ref,) = rest
        out_ref[0] = x


def _residual(x, gate, o, tt, g_final=None):
    n_seq, s, d = x.shape
    final = g_final is not None
    blk = pl.BlockSpec((1, tt, d), lambda b, i: (b, i, 0))
    in_specs = [blk, pl.BlockSpec((1, 1, d), lambda b, i: (b, 0, 0)), blk]
    args = [x, gate.reshape(n_seq, 1, d), o]
    if final:
        in_specs.append(pl.BlockSpec((1, d), lambda b, i: (0, 0)))
        args.append(g_final.reshape(1, d))
    return pl.pallas_call(
        functools.partial(_residual_kernel, final=final),
        grid=(n_seq, s // tt),
        in_specs=in_specs,
        out_specs=blk,
        out_shape=jax.ShapeDtypeStruct(x.shape, F32),
        compiler_params=_cparams(("parallel", "parallel")),
        name="residual",
    )(*args)


def _row_tile(s):
    return 256 if s % 256 == 0 else s


def _peer_tile(t):
    return 512 if t % 512 == 0 else 256


def _pad_lanes(x, n):
    return jnp.pad(x, [(0, 0)] * (x.ndim - 1) + [(0, n - x.shape[-1])])


def _prepare_weights(w_qkv_a, w_o_a, w_kv, w_f, b_f, w_q_b, w_o_b, w_pq, sub_keys, peer_u, peer_v):
    depth = w_pq.shape[0]

    def o_heads(w):
        w = w.reshape(N_HEADS, HEAD_DIM, D_MODEL)
        return jnp.pad(w, ((0, 0), (0, HEAD_LANES - HEAD_DIM), (0, 0))).astype(BF16)

    return dict(
        w_qkv=w_qkv_a[0].astype(BF16),
        w_o_a=o_heads(w_o_a[0]),
        w_kv=w_kv.astype(BF16),
        w_f=_pad_lanes(w_f, HEAD_LANES).astype(BF16),
        b_f=_pad_lanes(b_f, HEAD_LANES),
        w_q_b=w_q_b[0].astype(BF16),
        w_o_b=o_heads(w_o_b[0]),
        w_pq_t=[w_pq[l].T.astype(BF16) for l in range(depth)],
        keys=[jnp.swapaxes(sub_keys[l], 0, 1).reshape(2 * PEER_HEADS, N_KEYS, -1).astype(BF16)
              for l in range(depth)],
        u=[peer_u[l].astype(BF16) for l in range(depth)],
        v_t=[peer_v[l].T.astype(BF16) for l in range(depth)],
    )


def _peer_layer(x, h, gate, wts, l, g_final=None):
    n_seq, s, d = x.shape
    o = _peer(h.reshape(n_seq * s, d), wts["w_pq_t"][l], wts["keys"][l], wts["u"][l], wts["v_t"][l],
              tt=_peer_tile(n_seq * s), eb=8 * N_KEYS)
    return _residual(x, gate, o.reshape(n_seq, s, d), _row_tile(s), g_final=g_final)


def _with_cache(cache_heads, new_heads):
    return new_heads if cache_heads is None else jnp.concatenate([cache_heads, new_heads], axis=2)


def _trunk(x, mods0, mods1, modkv, caches, wts, g_norm, g_kv, g_final, rel_bias):
    n_seq, s, d = x.shape
    tt = _row_tile(s)
    sh_m, sc_m, gt_m, sh_c, sc_c, gt_c = jnp.split(mods0, 6, axis=-1)

    q_scale = HEAD_DIM ** -0.5
    heads = lambda scale, mode: (scale, mode)

    qh, kh, vh, k, v = _proj_heads(
        x, [(g_norm[0, 0], sh_m, sc_m, wts["w_qkv"])],
        [(0, 0, heads(q_scale, "plain")), (0, d, heads(1.0, "plain")), (0, 2 * d, heads(1.0, "plain")),
         (0, d, "nat"), (0, 2 * d, "nat")], tt)
    if caches is None:
        bias = _relbias_table(rel_bias, BAND, 2 * BAND, split=768, band_mask=True)
        oh = _band_attend_prompt(qh, kh, vh, bias)
        keep = min(BAND, s)
        new_ak, new_av = k[:, s - keep:], v[:, s - keep:]
    else:
        cache_a_k, cache_a_v = caches[0], caches[1]
        win = cache_a_k.shape[2]
        kh = _with_cache(_to_heads(cache_a_k[0].reshape(n_seq, win, d), _row_tile(win)), kh)
        vh = _with_cache(_to_heads(cache_a_v[0].reshape(n_seq, win, d), _row_tile(win)), vh)
        lanes = -(-(win + s) // 128) * 128
        bias = _relbias_table(rel_bias, s, win + s, split=(win + s + lanes - s) // 2, band_mask=False)
        oh = _band_attend_step(qh, kh, vh, bias)
        new_ak, new_av = k, v
    x, h = _oproj_residual(oh, wts["w_o_a"], x, gt_m, g_norm[0, 1], sh_c, sc_c, tt)
    x = _peer_layer(x, h, gt_c, wts, 0)

    sh_m, sc_m, gt_m, sh_c, sc_c, gt_c = jnp.split(mods1, 6, axis=-1)
    sh_kv, sc_kv = jnp.split(modkv, 2, axis=-1)
    (lf,) = _norm_proj(x, g_kv, sh_kv, sc_kv, wts["w_f"], (HEAD_LANES,), tt, bias_last=wts["b_f"])
    if caches is None:
        past = 0
        lf_all = lf
        tq = tk = 512 if s % 512 == 0 else s
        bs = 256 if s % 256 == 0 else s
    else:
        cache_b_k, cache_b_v, cache_b_logf = caches[2], caches[3], caches[4]
        past = cache_b_k.shape[1]
        lf_all = jnp.concatenate([_pad_lanes(cache_b_logf.astype(F32), HEAD_LANES), lf], axis=1)
        tq, tk = s, 512
        bs = s
    cum = _cumsum_seq(lf_all, bs)
    qh, kh, vh, bk, bv = _proj_heads(
        x, [(g_kv, sh_kv, sc_kv, wts["w_kv"]), (g_norm[1, 0], sh_m, sc_m, wts["w_q_b"])],
        [(1, 0, heads(q_scale, "q_aug")), (0, 0, heads(1.0, "k_aug")), (0, d, heads(1.0, "plain")),
         (0, 0, "nat"), (0, d, "nat")], tt, cum=cum[:, past:])
    if caches is not None:
        kh = _with_cache(_to_heads(cache_b_k.reshape(n_seq, past, d), _row_tile(past),
                                   mode="k_aug", cum=cum[:, :past]), kh)
        vh = _with_cache(_to_heads(cache_b_v.reshape(n_seq, past, d), _row_tile(past)), vh)
    oh = _fox_attend(qh, kh, vh, tq, tk, past)
    x, h = _oproj_residual(oh, wts["w_o_b"], x, gt_m, g_norm[1, 1], sh_c, sc_c, tt)
    y = _peer_layer(x, h, gt_c, wts, 1, g_final=g_final)

    hshape = (n_seq, -1, N_HEADS, HEAD_DIM)
    return (y, new_ak.reshape(hshape)[None], new_av.reshape(hshape)[None],
            bk.reshape(hshape), bv.reshape(hshape), lf[:, :, :N_HEADS])


def kernel(x_prompt, x_sample, c_prompt, c_sample, cache_a_k, cache_a_v, cache_b_k, cache_b_v, cache_b_logf, w_ada, b_ada, g_norm, w_qkv_a, rel_bias_a, w_o_a, w_ada_kv, b_ada_kv, g_kv, w_kv, w_f, b_f, w_q_b, w_o_b, w_pq, sub_keys, peer_u, peer_v, g_final):
    n_p, n_s = c_prompt.shape[0], c_sample.shape[0]
    n_c = -(-(n_p + n_s) // 16) * 16
    c_all = jnp.pad(jnp.concatenate([c_prompt, c_sample], axis=0), ((0, n_c - n_p - n_s), (0, 0)))
    mods0 = _cond_matmul(c_all, w_ada[0], b_ada[0])
    mods1 = _cond_matmul(c_all, w_ada[1], b_ada[1])
    modkv = _cond_matmul(c_all, w_ada_kv, b_ada_kv)
    wts = _prepare_weights(w_qkv_a, w_o_a, w_kv, w_f, b_f, w_q_b, w_o_b, w_pq, sub_keys, peer_u, peer_v)

    out_p = _trunk(x_prompt, mods0[:n_p], mods1[:n_p], modkv[:n_p], None,
                   wts, g_norm, g_kv, g_final, rel_bias_a[0])
    out_s = _trunk(x_sample, mods0[n_p:n_p + n_s], mods1[n_p:n_p + n_s], modkv[n_p:n_p + n_s],
                   (cache_a_k, cache_a_v, cache_b_k, cache_b_v, cache_b_logf),
                   wts, g_norm, g_kv, g_final, rel_bias_a[0])
    return (out_p[0], out_s[0]) + out_p[1:] + out_s[1:]
```

```python
import functools

import jax
import jax.numpy as jnp
from jax import lax
from jax.experimental import pallas as pl
from jax.experimental.pallas import tpu as pltpu

F32 = jnp.float32
BF16 = jnp.bfloat16

D_MODEL = 1024
HEAD_DIM = 64
N_HEADS = D_MODEL // HEAD_DIM
HEAD_LANES = 128
HEADS_PER_STEP = 2
CHUNK = 64
BAND_CHUNKS = 8
BAND = BAND_CHUNKS * CHUNK
REL_CLIP = 128
PEER_HEADS = 8
N_KEYS = 128
PEER_TOPK = 16
EPS = 1e-6
NEG_INF = -1e30

VMEM_LIMIT = 56 * 1024 * 1024


def _cparams(sem, vmem=VMEM_LIMIT):
    return pltpu.CompilerParams(dimension_semantics=sem, vmem_limit_bytes=vmem)


def _dot(a, b):
    return jnp.dot(a, b, preferred_element_type=F32)


def _dot_nt(a, b):
    return lax.dot_general(a, b, (((1,), (1,)), ((), ())), preferred_element_type=F32)


def _split3(x):
    hi = x.astype(BF16)
    r = x - hi.astype(F32)
    mid = r.astype(BF16)
    lo = (r - mid.astype(F32)).astype(BF16)
    return hi, mid, lo


def _cond_kernel(c_ref, w_ref, b_ref, o_ref):
    ch, cm, cl = _split3(c_ref[...])
    wh, wm, wl = _split3(w_ref[...])
    acc = _dot(ch, wh) + (_dot(ch, wm) + _dot(cm, wh)) + (_dot(ch, wl) + _dot(cl, wh) + _dot(cm, wm))
    o_ref[...] = acc + b_ref[...]


def _cond_matmul(c, w, b):
    m, d = c.shape
    n = w.shape[1]
    bn = 1024
    return pl.pallas_call(
        _cond_kernel,
        grid=(n // bn,),
        in_specs=[pl.BlockSpec((m, d), lambda j: (0, 0)),
                  pl.BlockSpec((d, bn), lambda j: (0, j)),
                  pl.BlockSpec((1, bn), lambda j: (0, j))],
        out_specs=pl.BlockSpec((m, bn), lambda j: (0, j)),
        out_shape=jax.ShapeDtypeStruct((m, n), F32),
        compiler_params=_cparams(("arbitrary",)),
        name="cond_matmul",
    )(c, w, b.reshape(1, n))


def _mod_norm(x, g, shift, scale):
    y = x * lax.rsqrt(jnp.mean(x * x, axis=-1, keepdims=True) + EPS)
    return (y * g) * (1.0 + scale) + shift


def _log_sigmoid(x):
    return jnp.minimum(x, 0.0) - jnp.log1p(jnp.exp(-jnp.abs(x)))


def _normproj_kernel(x_ref, g_ref, sh_ref, sc_ref, w_ref, *rest, splits, logsig_last):
    if logsig_last:
        bl_ref, o_refs = rest[0], rest[1:]
    else:
        o_refs = rest
    h = _mod_norm(x_ref[0], g_ref[...], sh_ref[0], sc_ref[0])
    out = _dot(h.astype(BF16), w_ref[...])
    off = 0
    for k, (o_ref, n) in enumerate(zip(o_refs, splits)):
        piece = out[:, off:off + n]
        if logsig_last and k == len(splits) - 1:
            piece = _log_sigmoid(piece + bl_ref[...])
        o_ref[0] = piece
        off += n


def _norm_proj(x, g, shift, scale, w, splits, tt, bias_last=None):
    n_seq, s, d = x.shape
    n = w.shape[1]
    logsig_last = bias_last is not None
    vec = pl.BlockSpec((1, 1, d), lambda b, i: (b, 0, 0))
    in_specs = [pl.BlockSpec((1, tt, d), lambda b, i: (b, i, 0)),
                pl.BlockSpec((1, d), lambda b, i: (0, 0)), vec, vec,
                pl.BlockSpec((d, n), lambda b, i: (0, 0))]
    args = [x, g.reshape(1, d), shift.reshape(n_seq, 1, d), scale.reshape(n_seq, 1, d), w]
    if logsig_last:
        in_specs.append(pl.BlockSpec((1, splits[-1]), lambda b, i: (0, 0)))
        args.append(bias_last.reshape(1, splits[-1]))
    return pl.pallas_call(
        functools.partial(_normproj_kernel, splits=tuple(splits), logsig_last=logsig_last),
        grid=(n_seq, s // tt),
        in_specs=in_specs,
        out_specs=[pl.BlockSpec((1, tt, k), lambda b, i: (b, i, 0)) for k in splits],
        out_shape=[jax.ShapeDtypeStruct((n_seq, s, k), F32) for k in splits],
        compiler_params=_cparams(("parallel", "parallel")),
        name="norm_proj",
    )(*args)


def _write_heads(x, o_ref, scale, mode, cum):
    tt = x.shape[0]
    lane = lax.broadcasted_iota(jnp.int32, (tt, HEAD_LANES), 1)
    for pair in range(N_HEADS // 2):
        col = x[:, pair * HEAD_LANES:(pair + 1) * HEAD_LANES]
        for par in range(2):
            h = 2 * pair + par
            c = col if par == 0 else pltpu.roll(col, HEAD_DIM, 1)
            if scale != 1.0:
                c = c * scale
            if mode == "plain":
                aug = jnp.zeros_like(c)
            else:
                cc = cum[:, h:h + 1]
                c1 = cc.astype(BF16).astype(F32)
                r = cc - c1
                c2 = r.astype(BF16).astype(F32)
                c3 = r - c2
                one = jnp.ones_like(c)
                if mode == "q_aug":
                    aug = jnp.where(lane == 64, c1, jnp.where(lane == 65, c2, jnp.where(lane == 66, c3, one)))
                else:
                    aug = jnp.where(lane == 67, -c1, jnp.where(lane == 68, -c2, jnp.where(lane == 69, -c3, one)))
                aug = jnp.where(lane < 70, aug, 0.0)
            o_ref[0, h] = jnp.where(lane < HEAD_DIM, c, aug).astype(BF16)


def _toheads_kernel(x_ref, *rest, scale, mode):
    if mode == "plain":
        (o_ref,) = rest
        cum = None
    else:
        cum_ref, o_ref = rest
        cum = cum_ref[0]
    _write_heads(x_ref[0], o_ref, scale, mode, cum)


def _projheads_kernel(*refs, n_mod, use_cum, plan):
    x_ref = refs[0]
    mods = [refs[1 + 4 * i:5 + 4 * i] for i in range(n_mod)]
    pos = 1 + 4 * n_mod
    cum = refs[pos][0] if use_cum else None
    out_refs = refs[pos + use_cum:]
    x = x_ref[0]
    y = x * lax.rsqrt(jnp.mean(x * x, axis=-1, keepdims=True) + EPS)
    outs = []
    for g_ref, sh_ref, sc_ref, w_ref in mods:
        h = (y * g_ref[...]) * (1.0 + sc_ref[0]) + sh_ref[0]
        outs.append(_dot(h.astype(BF16), w_ref[...]))
    for (m, off, kind), o_ref in zip(plan, out_refs):
        blk = outs[m][:, off:off + D_MODEL]
        if kind == "nat":
            o_ref[0] = blk
        else:
            _write_heads(blk, o_ref, kind[0], kind[1], cum)


def _proj_heads(x, mods, plan, tt, cum=None):
    n_seq, s, d = x.shape
    vec = pl.BlockSpec((1, 1, d), lambda b, i: (b, 0, 0))
    in_specs = [pl.BlockSpec((1, tt, d), lambda b, i: (b, i, 0))]
    args = [x]
    for g, shift, scale, w in mods:
        in_specs += [pl.BlockSpec((1, d), lambda b, i: (0, 0)), vec, vec,
                     pl.BlockSpec(w.shape, lambda b, i: (0, 0))]
        args += [g.reshape(1, d), shift.reshape(n_seq, 1, d), scale.reshape(n_seq, 1, d), w]
    if cum is not None:
        in_specs.append(pl.BlockSpec((1, tt, HEAD_LANES), lambda b, i: (b, i, 0)))
        args.append(cum)
    out_specs, out_shape = [], []
    for _, _, kind in plan:
        if kind == "nat":
            out_specs.append(pl.BlockSpec((1, tt, d), lambda b, i: (b, i, 0)))
            out_shape.append(jax.ShapeDtypeStruct((n_seq, s, d), F32))
        else:
            out_specs.append(pl.BlockSpec((1, N_HEADS, tt, HEAD_LANES), lambda b, i: (b, 0, i, 0)))
            out_shape.append(jax.ShapeDtypeStruct((n_seq, N_HEADS, s, HEAD_LANES), BF16))
    return pl.pallas_call(
        functools.partial(_projheads_kernel, n_mod=len(mods), use_cum=cum is not None, plan=tuple(plan)),
        grid=(n_seq, s // tt),
        in_specs=in_specs,
        out_specs=out_specs,
        out_shape=out_shape,
        compiler_params=_cparams(("parallel", "parallel")),
        name="proj_heads",
    )(*args)


def _to_heads(x, tt, scale=1.0, mode="plain", cum=None):
    n_seq, s, d = x.shape
    in_specs = [pl.BlockSpec((1, tt, d), lambda b, i: (b, i, 0))]
    args = [x]
    if mode != "plain":
        in_specs.append(pl.BlockSpec((1, tt, HEAD_LANES), lambda b, i: (b, i, 0)))
        args.append(cum)
    return pl.pallas_call(
        functools.partial(_toheads_kernel, scale=scale, mode=mode),
        grid=(n_seq, s // tt),
        in_specs=in_specs,
        out_specs=pl.BlockSpec((1, N_HEADS, tt, HEAD_LANES), lambda b, i: (b, 0, i, 0)),
        out_shape=jax.ShapeDtypeStruct((n_seq, N_HEADS, s, HEAD_LANES), BF16),
        compiler_params=_cparams(("parallel", "parallel")),
        name="to_heads",
    )(*args)


def _relbias_kernel(rb_ref, o_ref, *, n_q, n_k, split, band_mask):
    h = pl.program_id(0)
    lanes = o_ref.shape[2]
    idx = lax.broadcasted_iota(jnp.int32, (8, lanes), 1)
    dd = jnp.where(idx > split, BAND + (lanes - idx), BAND - idx)
    v = jnp.clip(dd, -REL_CLIP, REL_CLIP) + REL_CLIP

    def fill(k, t):
        return jnp.where(v == k, rb_ref[h, k], t)

    t = lax.fori_loop(0, 2 * REL_CLIP + 1, fill, jnp.zeros((8, lanes), F32))
    x = jnp.broadcast_to(t[0:1, :], (n_q, lanes))
    row = lax.broadcasted_iota(jnp.int32, (n_q, lanes), 0)
    bit = 1
    while bit < n_q:
        x = jnp.where((row & bit) != 0, pltpu.roll(x, bit, 1), x)
        bit *= 2
    if band_mask:
        col = lax.broadcasted_iota(jnp.int32, (n_q, lanes), 1)
        dc = BAND_CHUNKS + (row >> 6) - (col >> 6)
        x = jnp.where((dc >= 0) & (dc <= BAND_CHUNKS), x, NEG_INF)
    o_ref[0] = x


def _relbias_table(rel_bias, n_q, n_k, split, band_mask):
    lanes = -(-n_k // 128) * 128
    return pl.pallas_call(
        functools.partial(_relbias_kernel, n_q=n_q, n_k=n_k, split=split, band_mask=band_mask),
        grid=(N_HEADS,),
        in_specs=[pl.BlockSpec(memory_space=pltpu.SMEM)],
        out_specs=pl.BlockSpec((1, n_q, lanes), lambda h: (h, 0, 0)),
        out_shape=jax.ShapeDtypeStruct((N_HEADS, n_q, lanes), F32),
        compiler_params=_cparams(("arbitrary",)),
        name="relbias_table",
    )(rel_bias)


def _band_kernel(*refs, n_seg, first_seg_is_pad_at_zero):
    q_ref = refs[0]
    k_refs = refs[1:1 + n_seg]
    v_refs = refs[1 + n_seg:1 + 2 * n_seg]
    bias_ref = refs[1 + 2 * n_seg]
    o_ref = refs[2 + 2 * n_seg]
    for j in range(HEADS_PER_STEP):
        q = q_ref[0, j]
        scores = []
        off = 0
        for i in range(n_seg):
            k = k_refs[i][0, j]
            nk = k.shape[0]
            s = _dot_nt(q, k) + bias_ref[j, :, off:off + nk]
            if i == 0 and first_seg_is_pad_at_zero:
                s = jnp.where(pl.program_id(2) > 0, s, NEG_INF)
            scores.append(s)
            off += nk
        m = functools.reduce(jnp.maximum, [jnp.max(s, axis=-1, keepdims=True) for s in scores])
        ps = [jnp.exp(s - m) for s in scores]
        l = functools.reduce(jnp.add, [jnp.sum(p, axis=-1, keepdims=True) for p in ps])
        acc = functools.reduce(jnp.add, [_dot(p.astype(BF16), v_ref[0, j]) for p, v_ref in zip(ps, v_refs)])
        o_ref[0, j] = (acc / l).astype(BF16)


def _band_attend_prompt(qh, kh, vh, bias):
    n_seq, nh, s, hl = qh.shape
    tg = BAND
    blk = (1, HEADS_PER_STEP, tg, hl)
    cur = lambda h, b, g: (b, h, g, 0)
    prev = lambda h, b, g: (b, h, jnp.maximum(g - 1, 0), 0)
    return pl.pallas_call(
        functools.partial(_band_kernel, n_seg=2, first_seg_is_pad_at_zero=True),
        grid=(nh // HEADS_PER_STEP, n_seq, s // tg),
        in_specs=[pl.BlockSpec(blk, cur),
                  pl.BlockSpec(blk, prev), pl.BlockSpec(blk, cur),
                  pl.BlockSpec(blk, prev), pl.BlockSpec(blk, cur),
                  pl.BlockSpec((HEADS_PER_STEP, tg, 2 * tg), lambda h, b, g: (h, 0, 0))],
        out_specs=pl.BlockSpec(blk, cur),
        out_shape=jax.ShapeDtypeStruct(qh.shape, BF16),
        compiler_params=_cparams(("arbitrary", "arbitrary", "arbitrary")),
        name="band_attend_prompt",
    )(qh, kh, kh, vh, vh, bias)


def _band_attend_step(qh, kh, vh, bias):
    n_seq, nh, sq, hl = qh.shape
    sk = kh.shape[2]
    hps = HEADS_PER_STEP
    return pl.pallas_call(
        functools.partial(_band_kernel, n_seg=1, first_seg_is_pad_at_zero=False),
        grid=(nh // hps, n_seq, 1),
        in_specs=[pl.BlockSpec((1, hps, sq, hl), lambda h, b, g: (b, h, 0, 0)),
                  pl.BlockSpec((1, hps, sk, hl), lambda h, b, g: (b, h, 0, 0)),
                  pl.BlockSpec((1, hps, sk, hl), lambda h, b, g: (b, h, 0, 0)),
                  pl.BlockSpec((hps, sq, bias.shape[2]), lambda h, b, g: (h, 0, 0))],
        out_specs=pl.BlockSpec((1, hps, sq, hl), lambda h, b, g: (b, h, 0, 0)),
        out_shape=jax.ShapeDtypeStruct(qh.shape, BF16),
        compiler_params=_cparams(("arbitrary", "arbitrary", "arbitrary")),
        name="band_attend_step",
    )(qh, kh, vh, bias)


def _fox_kernel(q_ref, k_ref, v_ref, o_ref, *, tq, tk, past):
    qi = pl.program_id(2)
    heads = range(HEADS_PER_STEP)
    qs = [q_ref[0, j] for j in heads]

    def update(carry, s, v):
        m, l, acc = carry
        m_new = jnp.maximum(m, jnp.max(s, axis=-1, keepdims=True))
        alpha = jnp.exp(m - m_new)
        p = jnp.exp(s - m_new)
        l = alpha * l + jnp.sum(p, axis=-1, keepdims=True)
        acc = alpha * acc + _dot(p.astype(BF16), v)
        return m_new, l, acc

    def full_tile(t, carries):
        rows = pl.ds(pl.multiple_of(t * tk, tk), tk)
        return tuple(update(carries[j], _dot_nt(qs[j], k_ref[0, j, rows, :]), v_ref[0, j, rows, :])
                     for j in heads)

    init = tuple((jnp.full((tq, 1), NEG_INF, F32), jnp.zeros((tq, 1), F32), jnp.zeros((tq, HEAD_LANES), F32))
                 for _ in heads)
    q_start = past + qi * tq
    carries = lax.fori_loop(0, q_start // tk, full_tile, init)
    rows = pl.ds(pl.multiple_of(q_start, tq), tq)
    row = lax.broadcasted_iota(jnp.int32, (tq, tq), 0)
    col = lax.broadcasted_iota(jnp.int32, (tq, tq), 1)
    for j in heads:
        s = jnp.where(col <= row, _dot_nt(qs[j], k_ref[0, j, rows, :]), NEG_INF)
        m, l, acc = update(carries[j], s, v_ref[0, j, rows, :])
        o_ref[0, j] = (acc / l).astype(BF16)


def _fox_attend(qh, kh, vh, tq, tk, past):
    n_seq, nh, sq, hl = qh.shape
    sk = kh.shape[2]
    hps = HEADS_PER_STEP
    return pl.pallas_call(
        functools.partial(_fox_kernel, tq=tq, tk=tk, past=past),
        grid=(n_seq, nh // hps, sq // tq),
        in_specs=[pl.BlockSpec((1, hps, tq, hl), lambda b, h, i: (b, h, i, 0)),
                  pl.BlockSpec((1, hps, sk, hl), lambda b, h, i: (b, h, 0, 0)),
                  pl.BlockSpec((1, hps, sk, hl), lambda b, h, i: (b, h, 0, 0))],
        out_specs=pl.BlockSpec((1, hps, tq, hl), lambda b, h, i: (b, h, i, 0)),
        out_shape=jax.ShapeDtypeStruct(qh.shape, BF16),
        compiler_params=_cparams(("arbitrary", "arbitrary", "arbitrary")),
        name="fox_attend",
    )(qh, kh, vh)


def _cumsum_kernel(x_ref, o_ref, *, bs):
    n_blk = x_ref.shape[1] // bs
    r = lax.broadcasted_iota(jnp.int32, (bs, bs), 0)
    c = lax.broadcasted_iota(jnp.int32, (bs, bs), 1)
    tri = jnp.where(c <= r, 1.0, 0.0).astype(BF16)

    def body(i, carry):
        start = pl.multiple_of(i * bs, bs)
        hi, mid, lo = _split3(x_ref[0, pl.ds(start, bs), :])
        out = (_dot(tri, hi) + _dot(tri, mid) + _dot(tri, lo)) + carry
        o_ref[0, pl.ds(start, bs), :] = out
        return out[bs - 1:bs, :]

    lax.fori_loop(0, n_blk, body, jnp.zeros((1, x_ref.shape[2]), F32))


def _cumsum_seq(x, bs):
    n_seq, s, w = x.shape
    return pl.pallas_call(
        functools.partial(_cumsum_kernel, bs=bs),
        grid=(n_seq,),
        in_specs=[pl.BlockSpec((1, s, w), lambda b: (b, 0, 0))],
        out_specs=pl.BlockSpec((1, s, w), lambda b: (b, 0, 0)),
        out_shape=jax.ShapeDtypeStruct(x.shape, F32),
        compiler_params=_cparams(("parallel",)),
        name="cumsum_seq",
    )(x)


def _oproj_kernel(o_ref, w_ref, x_ref, gate_ref, g_ref, sh_ref, sc_ref, out_ref, h_ref):
    acc = _dot(o_ref[0, 0], w_ref[0])
    for h in range(1, N_HEADS):
        acc = acc + _dot(o_ref[0, h], w_ref[h])
    x = x_ref[0] + gate_ref[0] * acc
    out_ref[0] = x
    h_ref[0] = _mod_norm(x, g_ref[...], sh_ref[0], sc_ref[0]).astype(BF16)


def _oproj_residual(oh, w_heads, x, gate, g, shift, scale, tt):
    n_seq, s, d = x.shape
    vec = pl.BlockSpec((1, 1, d), lambda b, i: (b, 0, 0))
    blk = pl.BlockSpec((1, tt, d), lambda b, i: (b, i, 0))
    return pl.pallas_call(
        _oproj_kernel,
        grid=(n_seq, s // tt),
        in_specs=[pl.BlockSpec((1, N_HEADS, tt, HEAD_LANES), lambda b, i: (b, 0, i, 0)),
                  pl.BlockSpec((N_HEADS, HEAD_LANES, d), lambda b, i: (0, 0, 0)),
                  blk, vec, pl.BlockSpec((1, d), lambda b, i: (0, 0)), vec, vec],
        out_specs=[blk, blk],
        out_shape=[jax.ShapeDtypeStruct(x.shape, F32), jax.ShapeDtypeStruct(x.shape, BF16)],
        compiler_params=_cparams(("parallel", "parallel")),
        name="oproj_residual",
    )(oh, w_heads, x, gate.reshape(n_seq, 1, d), g.reshape(1, d),
      shift.reshape(n_seq, 1, d), scale.reshape(n_seq, 1, d))


LANE_CHUNK = 128
SUB_ROWS = 8
PACK_ROWS = 16
RETRIEVE_UNROLL = 2


def _topk_rows(s, row, exact):
    rank = jnp.full(s.shape, float(PEER_TOPK), F32)
    tops = []
    for r in range(PEER_TOPK):
        m = jnp.max(s, axis=0, keepdims=True)
        sel = s == m
        if exact:
            first = jnp.min(jnp.where(sel, row, float(N_KEYS)), axis=0, keepdims=True)
            sel = row == first
        rank = jnp.where(sel, float(r), rank)
        s = jnp.where(sel, -jnp.inf, s)
        tops.append(m)
    count = jnp.sum(jnp.where(rank < float(PEER_TOPK), 1.0, 0.0), axis=0, keepdims=True)
    return rank, tops, count


def _stack16(rows, row16):
    out = jnp.zeros(row16.shape, F32)
    for b, v in enumerate(rows):
        out = jnp.where(row16 == float(b), v, out)
    return out


def _select_pairs(top1, top2, row8, exact):
    lanes = top1[0].shape
    row16 = jnp.concatenate([row8, row8 + 8.0], axis=0)
    t2 = _stack16(top2, row16)
    t2_lo, t2_hi = t2[0:8], t2[8:16]
    cands, poss = [], []
    for a in range(PEER_TOPK):
        nb = PEER_TOPK // (a + 1)
        halves = [(t2_lo, 0)] + ([(t2_hi, 8)] if nb > 8 else [])
        for t2h, b0 in halves:
            c = top1[a] + t2h
            valid = (row8 + float(b0)) < float(nb)
            cands.append(jnp.where(valid, c, -jnp.inf))
            poss.append(jnp.where(valid, row8 + float(a * PEER_TOPK + b0), 1e9))
    orig = list(cands)
    picked = [jnp.zeros(c.shape, F32) for c in cands]
    for _ in range(PEER_TOPK):
        m = jnp.max(functools.reduce(jnp.maximum, cands), axis=0, keepdims=True)
        sels = [c == m for c in cands]
        if exact:
            hit = [jnp.where(sl, p, 1e9) for sl, p in zip(sels, poss)]
            first = jnp.min(functools.reduce(jnp.minimum, hit), axis=0, keepdims=True)
            sels = [p == first for p in poss]
        cands = [jnp.where(sl, -jnp.inf, c) for sl, c in zip(sels, cands)]
        picked = [jnp.where(sl, 1.0, pk) for sl, pk in zip(sels, picked)]
    m0 = top1[0] + top2[0]
    z = jnp.zeros(lanes, F32)
    count = jnp.zeros(lanes, F32)
    n_sel = []
    k = 0
    for a in range(PEER_TOPK):
        nb = PEER_TOPK // (a + 1)
        n_a = jnp.zeros(lanes, F32)
        for _ in range(2 if nb > 8 else 1):
            n_a = n_a + jnp.sum(picked[k], axis=0, keepdims=True)
            e = jnp.where(picked[k] > 0.5, jnp.exp(orig[k] - m0), 0.0)
            z = z + jnp.sum(e, axis=0, keepdims=True)
            k += 1
        n_sel.append(n_a)
        count = count + n_a
    return n_sel, z, count


def _gelu2(x):
    c0 = 0.7978845608028654
    return x + x * jnp.tanh(x * (c0 + (c0 * 0.044715) * (x * x)))


def _peer_kernel(h_ref, wq_ref, keys_ref, u_ref, vt_ref, out_ref,
                 xt_s, acc_s, ht_s, at_s, sc_s, rf_s, e1_s, n_s, *, tt, eb):
    e = pl.program_id(1)
    n_chunks = tt // LANE_CHUNK
    rows_per_blk = eb // N_KEYS

    @pl.when(e == 0)
    def _retrieve():
        xt = h_ref[...].astype(F32).T.astype(BF16)
        xt_s[...] = xt
        acc_s[...] = jnp.zeros_like(acc_s)
        qt = _dot(wq_ref[...], xt)
        for hp in range(2 * PEER_HEADS):
            sc_s[hp] = _dot(keys_ref[hp], qt[hp * N_KEYS:(hp + 1) * N_KEYS].astype(BF16))

        def select(hc, exact):
            h = hc // n_chunks
            c0 = pl.multiple_of((hc % n_chunks) * LANE_CHUNK, LANE_CHUNK)
            lanes = pl.ds(c0, LANE_CHUNK)
            row = lax.broadcasted_iota(jnp.int32, (N_KEYS, LANE_CHUNK), 0).astype(F32)
            row8 = lax.broadcasted_iota(jnp.int32, (8, LANE_CHUNK), 0).astype(F32)
            s1 = sc_s[2 * h, :, lanes]
            s2 = sc_s[2 * h + 1, :, lanes]
            rank1, top1, cnt1 = _topk_rows(s1, row, exact)
            rank2, top2, cnt2 = _topk_rows(s2, row, exact)
            n_sel, z, cnt3 = _select_pairs(top1, top2, row8, exact)
            n_dense = jnp.zeros((N_KEYS, LANE_CHUNK), F32)
            for a in range(PEER_TOPK):
                n_dense = jnp.where(rank1 == float(a), n_sel[a], n_dense)
            n_s[h, :, lanes] = n_dense
            e1_s[h, :, lanes] = jnp.exp(s1 - top1[0])
            f = jnp.exp(s2 - top2[0]) * (0.5 / z)
            for rt in range(N_KEYS // SUB_ROWS):
                keys = slice(rt * SUB_ROWS, (rt + 1) * SUB_ROWS)
                rf_s[hc % n_chunks, rt, 2 * h] = rank2[keys]
                rf_s[hc % n_chunks, rt, 2 * h + 1] = f[keys]
            return jnp.max(jnp.maximum(jnp.maximum(cnt1, cnt2), cnt3))

        def per_head_pair(i, _):
            units = [RETRIEVE_UNROLL * i + k for k in range(RETRIEVE_UNROLL)]
            most = [select(hc, exact=False) for hc in units]
            for hc, m in zip(units, most):
                @pl.when(m > PEER_TOPK + 0.5)
                def _ties():
                    select(hc, exact=True)
            return 0

        lax.fori_loop(0, PEER_HEADS * n_chunks // RETRIEVE_UNROLL, per_head_pair, 0)

    ht_s[...] = _dot(u_ref[...], xt_s[...])

    def per_chunk(c, _):
        lanes = pl.ds(pl.multiple_of(c * LANE_CHUNK, LANE_CHUNK), LANE_CHUNK)
        grp = pl.ds(pl.multiple_of(e * rows_per_blk, rows_per_blk), rows_per_blk)
        n_grp = [n_s[h, grp, lanes] for h in range(PEER_HEADS)]
        e_grp = [e1_s[h, grp, lanes] for h in range(PEER_HEADS)]
        tile = (SUB_ROWS, LANE_CHUNK)
        for il in range(rows_per_blk):
            n8 = [jnp.broadcast_to(n_grp[h][il:il + 1], tile) for h in range(PEER_HEADS)]
            e8 = [jnp.broadcast_to(e_grp[h][il:il + 1], tile) for h in range(PEER_HEADS)]
            for rp in range(N_KEYS // PACK_ROWS):
                halves = []
                for rt in (2 * rp, 2 * rp + 1):
                    w = None
                    for h in range(PEER_HEADS):
                        t = jnp.where(rf_s[c, rt, 2 * h] < n8[h], rf_s[c, rt, 2 * h + 1], 0.0) * e8[h]
                        w = t if w is None else w + t
                    rows = pl.ds(il * N_KEYS + rt * SUB_ROWS, SUB_ROWS)
                    halves.append(_gelu2(ht_s[rows, lanes]) * w)
                rows = pl.ds(il * N_KEYS + rp * PACK_ROWS, PACK_ROWS)
                at_s[rows, lanes] = jnp.concatenate(halves, axis=0).astype(BF16)
        return 0

    lax.fori_loop(0, n_chunks, per_chunk, 0)
    acc_s[...] += _dot(vt_ref[...], at_s[...])

    @pl.when(e == pl.num_programs(1) - 1)
    def _emit():
        out_ref[...] = acc_s[...].T


def _peer(h_flat, wq_t, keys, u, v_t, tt, eb):
    t, d = h_flat.shape
    n_exp = u.shape[0]
    assert t % tt == 0 and eb == 8 * N_KEYS and n_exp % eb == 0
    scr = pltpu.VMEM
    return pl.pallas_call(
        functools.partial(_peer_kernel, tt=tt, eb=eb),
        grid=(t // tt, n_exp // eb),
        in_specs=[pl.BlockSpec((tt, d), lambda i, e: (i, 0)),
                  pl.BlockSpec(wq_t.shape, lambda i, e: (0, 0)),
                  pl.BlockSpec(keys.shape, lambda i, e: (0, 0, 0)),
                  pl.BlockSpec((eb, d), lambda i, e: (e, 0)),
                  pl.BlockSpec((d, eb), lambda i, e: (0, e))],
        out_specs=pl.BlockSpec((tt, d), lambda i, e: (i, 0)),
        out_shape=jax.ShapeDtypeStruct((t, d), F32),
        scratch_shapes=[scr((d, tt), BF16), scr((d, tt), F32), scr((eb, tt), F32), scr((eb, tt), BF16),
                        scr((2 * PEER_HEADS, N_KEYS, tt), F32),
                        scr((tt // LANE_CHUNK, N_KEYS // SUB_ROWS, 2 * PEER_HEADS, SUB_ROWS, LANE_CHUNK), F32),
                        scr((PEER_HEADS, N_KEYS, tt), F32), scr((PEER_HEADS, N_KEYS, tt), F32)],
        compiler_params=_cparams(("parallel", "arbitrary")),
        name="peer_dense",
    )(h_flat, wq_t, keys, u, v_t)


def _residual_kernel(x_ref, gate_ref, o_ref, *rest, final):
    x = x_ref[0] + gate_ref[0] * o_ref[0]
    if final:
        g_ref, out_ref = rest
        out_ref[0] = (x * lax.rsqrt(jnp.mean(x * x, axis=-1, keepdims=True) + EPS)) * g_ref[...]
    else:
        (out_ref,) = rest
        out_ref[0] = x


def _residual(x, gate, o, tt, g_final=None):
    n_seq, s, d = x.shape
    final = g_final is not None
    blk = pl.BlockSpec((1, tt, d), lambda b, i: (b, i, 0))
    in_specs = [blk, pl.BlockSpec((1, 1, d), lambda b, i: (b, 0, 0)), blk]
    args = [x, gate.reshape(n_seq, 1, d), o]
    if final:
        in_specs.append(pl.BlockSpec((1, d), lambda b, i: (0, 0)))
        args.append(g_final.reshape(1, d))
    return pl.pallas_call(
        functools.partial(_residual_kernel, final=final),
        grid=(n_seq, s // tt),
        in_specs=in_specs,
        out_specs=blk,
        out_shape=jax.ShapeDtypeStruct(x.shape, F32),
        compiler_params=_cparams(("parallel", "parallel")),
        name="residual",
    )(*args)


def _row_tile(s):
    return 256 if s % 256 == 0 else s


def _peer_tile(t):
    return 512 if t % 512 == 0 else 256


def _pad_lanes(x, n):
    return jnp.pad(x, [(0, 0)] * (x.ndim - 1) + [(0, n - x.shape[-1])])


def _prepare_weights(w_qkv_a, w_o_a, w_kv, w_f, b_f, w_q_b, w_o_b, w_pq, sub_keys, peer_u, peer_v):
    depth = w_pq.shape[0]

    def o_heads(w):
        w = w.reshape(N_HEADS, HEAD_DIM, D_MODEL)
        return jnp.pad(w, ((0, 0), (0, HEAD_LANES - HEAD_DIM), (0, 0))).astype(BF16)

    return dict(
        w_qkv=w_qkv_a[0].astype(BF16),
        w_o_a=o_heads(w_o_a[0]),
        w_kv=w_kv.astype(BF16),
        w_f=_pad_lanes(w_f, HEAD_LANES).astype(BF16),
        b_f=_pad_lanes(b_f, HEAD_LANES),
        w_q_b=w_q_b[0].astype(BF16),
        w_o_b=o_heads(w_o_b[0]),
        w_pq_t=[w_pq[l].T.astype(BF16) for l in range(depth)],
        keys=[jnp.swapaxes(sub_keys[l], 0, 1).reshape(2 * PEER_HEADS, N_KEYS, -1).astype(BF16)
              for l in range(depth)],
        u=[peer_u[l].astype(BF16) for l in range(depth)],
        v_t=[peer_v[l].T.astype(BF16) for l in range(depth)],
    )


def _peer_layer(x, h, gate, wts, l, g_final=None):
    n_seq, s, d = x.shape
    o = _peer(h.reshape(n_seq * s, d), wts["w_pq_t"][l], wts["keys"][l], wts["u"][l], wts["v_t"][l],
              tt=_peer_tile(n_seq * s), eb=8 * N_KEYS)
    return _residual(x, gate, o.reshape(n_seq, s, d), _row_tile(s), g_final=g_final)


def _with_cache(cache_heads, new_heads):
    return new_heads if cache_heads is None else jnp.concatenate([cache_heads, new_heads], axis=2)


def _trunk(x, mods0, mods1, modkv, caches, wts, g_norm, g_kv, g_final, rel_bias):
    n_seq, s, d = x.shape
    tt = _row_tile(s)
    sh_m, sc_m, gt_m, sh_c, sc_c, gt_c = jnp.split(mods0, 6, axis=-1)

    q_scale = HEAD_DIM ** -0.5
    heads = lambda scale, mode: (scale, mode)

    qh, kh, vh, k, v = _proj_heads(
        x, [(g_norm[0, 0], sh_m, sc_m, wts["w_qkv"])],
        [(0, 0, heads(q_scale, "plain")), (0, d, heads(1.0, "plain")), (0, 2 * d, heads(1.0, "plain")),
         (0, d, "nat"), (0, 2 * d, "nat")], tt)
    if caches is None:
        bias = _relbias_table(rel_bias, BAND, 2 * BAND, split=768, band_mask=True)
        oh = _band_attend_prompt(qh, kh, vh, bias)
        keep = min(BAND, s)
        new_ak, new_av = k[:, s - keep:], v[:, s - keep:]
    else:
        cache_a_k, cache_a_v = caches[0], caches[1]
        win = cache_a_k.shape[2]
        kh = _with_cache(_to_heads(cache_a_k[0].reshape(n_seq, win, d), _row_tile(win)), kh)
        vh = _with_cache(_to_heads(cache_a_v[0].reshape(n_seq, win, d), _row_tile(win)), vh)
        lanes = -(-(win + s) // 128) * 128
        bias = _relbias_table(rel_bias, s, win + s, split=(win + s + lanes - s) // 2, band_mask=False)
        oh = _band_attend_step(qh, kh, vh, bias)
        new_ak, new_av = k, v
    x, h = _oproj_residual(oh, wts["w_o_a"], x, gt_m, g_norm[0, 1], sh_c, sc_c, tt)
    x = _peer_layer(x, h, gt_c, wts, 0)

    sh_m, sc_m, gt_m, sh_c, sc_c, gt_c = jnp.split(mods1, 6, axis=-1)
    sh_kv, sc_kv = jnp.split(modkv, 2, axis=-1)
    (lf,) = _norm_proj(x, g_kv, sh_kv, sc_kv, wts["w_f"], (HEAD_LANES,), tt, bias_last=wts["b_f"])
    if caches is None:
        past = 0
        lf_all = lf
        tq = tk = 512 if s % 512 == 0 else s
        bs = 256 if s % 256 == 0 else s
    else:
        cache_b_k, cache_b_v, cache_b_logf = caches[2], caches[3], caches[4]
        past = cache_b_k.shape[1]
        lf_all = jnp.concatenate([_pad_lanes(cache_b_logf.astype(F32), HEAD_LANES), lf], axis=1)
        tq, tk = s, 512
        bs = s
    cum = _cumsum_seq(lf_all, bs)
    qh, kh, vh, bk, bv = _proj_heads(
        x, [(g_kv, sh_kv, sc_kv, wts["w_kv"]), (g_norm[1, 0], sh_m, sc_m, wts["w_q_b"])],
        [(1, 0, heads(q_scale, "q_aug")), (0, 0, heads(1.0, "k_aug")), (0, d, heads(1.0, "plain")),
         (0, 0, "nat"), (0, d, "nat")], tt, cum=cum[:, past:])
    if caches is not None:
        kh = _with_cache(_to_heads(cache_b_k.reshape(n_seq, past, d), _row_tile(past),
                                   mode="k_aug", cum=cum[:, :past]), kh)
        vh = _with_cache(_to_heads(cache_b_v.reshape(n_seq, past, d), _row_tile(past)), vh)
    oh = _fox_attend(qh, kh, vh, tq, tk, past)
    x, h = _oproj_residual(oh, wts["w_o_b"], x, gt_m, g_norm[1, 1], sh_c, sc_c, tt)
    y = _peer_layer(x, h, gt_c, wts, 1, g_final=g_final)

    hshape = (n_seq, -1, N_HEADS, HEAD_DIM)
    return (y, new_ak.reshape(hshape)[None], new_av.reshape(hshape)[None],
            bk.reshape(hshape), bv.reshape(hshape), lf[:, :, :N_HEADS])


def kernel(x_prompt, x_sample, c_prompt, c_sample, cache_a_k, cache_a_v, cache_b_k, cache_b_v, cache_b_logf, w_ada, b_ada, g_norm, w_qkv_a, rel_bias_a, w_o_a, w_ada_kv, b_ada_kv, g_kv, w_kv, w_f, b_f, w_q_b, w_o_b, w_pq, sub_keys, peer_u, peer_v, g_final):
    n_p, n_s = c_prompt.shape[0], c_sample.shape[0]
    n_c = -(-(n_p + n_s) // 16) * 16
    c_all = jnp.pad(jnp.concatenate([c_prompt, c_sample], axis=0), ((0, n_c - n_p - n_s), (0, 0)))
    mods0 = _cond_matmul(c_all, w_ada[0], b_ada[0])
    mods1 = _cond_matmul(c_all, w_ada[1], b_ada[1])
    modkv = _cond_matmul(c_all, w_ada_kv, b_ada_kv)
    wts = _prepare_weights(w_qkv_a, w_o_a, w_kv, w_f, b_f, w_q_b, w_o_b, w_pq, sub_keys, peer_u, peer_v)

    out_p = _trunk(x_prompt, mods0[:n_p], mods1[:n_p], modkv[:n_p], None,
                   wts, g_norm, g_kv, g_final, rel_bias_a[0])
    out_s = _trunk(x_sample, mods0[n_p:n_p + n_s], mods1[n_p:n_p + n_s], modkv[n_p:n_p + n_s],
                   (cache_a_k, cache_a_v, cache_b_k, cache_b_v, cache_b_logf),
                   wts, g_norm, g_kv, g_final, rel_bias_a[0])
    return (out_p[0], out_s[0]) + out_p[1:] + out_s[1:]
```

```python
import functools

import jax
import jax.numpy as jnp
from jax import lax
from jax.experimental import pallas as pl
from jax.experimental.pallas import tpu as pltpu

F32 = jnp.float32
BF16 = jnp.bfloat16

D_MODEL = 1024
HEAD_DIM = 64
N_HEADS = D_MODEL // HEAD_DIM
HEAD_LANES = 128
HEADS_PER_STEP = 2
FOX_HEADS_PER_STEP = 4
CHUNK = 64
BAND_CHUNKS = 8
BAND = BAND_CHUNKS * CHUNK
REL_CLIP = 128
PEER_HEADS = 8
N_KEYS = 128
PEER_TOPK = 16
EPS = 1e-6
NEG_INF = -1e30

VMEM_LIMIT = 56 * 1024 * 1024


def _cparams(sem, vmem=VMEM_LIMIT):
    return pltpu.CompilerParams(dimension_semantics=sem, vmem_limit_bytes=vmem)


def _dot(a, b):
    return jnp.dot(a, b, preferred_element_type=F32)


def _dot_nt(a, b):
    return lax.dot_general(a, b, (((1,), (1,)), ((), ())), preferred_element_type=F32)


def _split3(x):
    hi = x.astype(BF16)
    r = x - hi.astype(F32)
    mid = r.astype(BF16)
    lo = (r - mid.astype(F32)).astype(BF16)
    return hi, mid, lo


def _cond_kernel(c_ref, w_ref, b_ref, o_ref):
    ch, cm, cl = _split3(c_ref[...])
    wh, wm, wl = _split3(w_ref[...])
    acc = _dot(ch, wh) + (_dot(ch, wm) + _dot(cm, wh)) + (_dot(ch, wl) + _dot(cl, wh) + _dot(cm, wm))
    o_ref[...] = acc + b_ref[...]


def _cond_matmul(c, w, b):
    m, d = c.shape
    n = w.shape[1]
    bn = 1024
    return pl.pallas_call(
        _cond_kernel,
        grid=(n // bn,),
        in_specs=[pl.BlockSpec((m, d), lambda j: (0, 0)),
                  pl.BlockSpec((d, bn), lambda j: (0, j)),
                  pl.BlockSpec((1, bn), lambda j: (0, j))],
        out_specs=pl.BlockSpec((m, bn), lambda j: (0, j)),
        out_shape=jax.ShapeDtypeStruct((m, n), F32),
        compiler_params=_cparams(("arbitrary",)),
        name="cond_matmul",
    )(c, w, b.reshape(1, n))


def _mod_norm(x, g, shift, scale):
    y = x * lax.rsqrt(jnp.mean(x * x, axis=-1, keepdims=True) + EPS)
    return (y * g) * (1.0 + scale) + shift


def _log_sigmoid(x):
    return jnp.minimum(x, 0.0) - jnp.log1p(jnp.exp(-jnp.abs(x)))


def _normproj_kernel(x_ref, g_ref, sh_ref, sc_ref, w_ref, *rest, splits, logsig_last):
    if logsig_last:
        bl_ref, o_refs = rest[0], rest[1:]
    else:
        o_refs = rest
    h = _mod_norm(x_ref[0], g_ref[...], sh_ref[0], sc_ref[0])
    out = _dot(h.astype(BF16), w_ref[...])
    off = 0
    for k, (o_ref, n) in enumerate(zip(o_refs, splits)):
        piece = out[:, off:off + n]
        if logsig_last and k == len(splits) - 1:
            piece = _log_sigmoid(piece + bl_ref[...])
        o_ref[0] = piece
        off += n


def _norm_proj(x, g, shift, scale, w, splits, tt, bias_last=None):
    n_seq, s, d = x.shape
    n = w.shape[1]
    logsig_last = bias_last is not None
    vec = pl.BlockSpec((1, 1, d), lambda b, i: (b, 0, 0))
    in_specs = [pl.BlockSpec((1, tt, d), lambda b, i: (b, i, 0)),
                pl.BlockSpec((1, d), lambda b, i: (0, 0)), vec, vec,
                pl.BlockSpec((d, n), lambda b, i: (0, 0))]
    args = [x, g.reshape(1, d), shift.reshape(n_seq, 1, d), scale.reshape(n_seq, 1, d), w]
    if logsig_last:
        in_specs.append(pl.BlockSpec((1, splits[-1]), lambda b, i: (0, 0)))
        args.append(bias_last.reshape(1, splits[-1]))
    return pl.pallas_call(
        functools.partial(_normproj_kernel, splits=tuple(splits), logsig_last=logsig_last),
        grid=(n_seq, s // tt),
        in_specs=in_specs,
        out_specs=[pl.BlockSpec((1, tt, k), lambda b, i: (b, i, 0)) for k in splits],
        out_shape=[jax.ShapeDtypeStruct((n_seq, s, k), F32) for k in splits],
        compiler_params=_cparams(("parallel", "parallel")),
        name="norm_proj",
    )(*args)


def _write_heads(x, o_ref, scale, mode, cum):
    tt = x.shape[0]
    lane = lax.broadcasted_iota(jnp.int32, (tt, HEAD_LANES), 1)
    for pair in range(N_HEADS // 2):
        col = x[:, pair * HEAD_LANES:(pair + 1) * HEAD_LANES]
        for par in range(2):
            h = 2 * pair + par
            c = col if par == 0 else pltpu.roll(col, HEAD_DIM, 1)
            if scale != 1.0:
                c = c * scale
            if mode == "plain":
                aug = jnp.zeros_like(c)
            elif mode == "ones":
                aug = jnp.where(lane == HEAD_DIM, 1.0, 0.0)
            else:
                cc = cum[:, h:h + 1]
                c1 = cc.astype(BF16).astype(F32)
                r = cc - c1
                c2 = r.astype(BF16).astype(F32)
                c3 = r - c2
                one = jnp.ones_like(c)
                if mode == "q_aug":
                    aug = jnp.where(lane == 64, c1, jnp.where(lane == 65, c2, jnp.where(lane == 66, c3, one)))
                else:
                    aug = jnp.where(lane == 67, -c1, jnp.where(lane == 68, -c2, jnp.where(lane == 69, -c3, one)))
                aug = jnp.where(lane < 70, aug, 0.0)
            o_ref[0, h] = jnp.where(lane < HEAD_DIM, c, aug).astype(BF16)


def _toheads_kernel(x_ref, *rest, scale, mode):
    if mode in ("q_aug", "k_aug"):
        cum_ref, o_ref = rest
        cum = cum_ref[0]
    else:
        (o_ref,) = rest
        cum = None
    _write_heads(x_ref[0], o_ref, scale, mode, cum)


def _projheads_kernel(*refs, n_mod, use_cum, plan):
    x_ref = refs[0]
    mods = [refs[1 + 4 * i:5 + 4 * i] for i in range(n_mod)]
    pos = 1 + 4 * n_mod
    cum = refs[pos][0] if use_cum else None
    out_refs = refs[pos + use_cum:]
    x = x_ref[0]
    y = x * lax.rsqrt(jnp.mean(x * x, axis=-1, keepdims=True) + EPS)
    outs = []
    for g_ref, sh_ref, sc_ref, w_ref in mods:
        h = (y * g_ref[...]) * (1.0 + sc_ref[0]) + sh_ref[0]
        outs.append(_dot(h.astype(BF16), w_ref[...]))
    for (m, off, kind), o_ref in zip(plan, out_refs):
        blk = outs[m][:, off:off + D_MODEL]
        if kind == "nat":
            o_ref[0] = blk
        else:
            _write_heads(blk, o_ref, kind[0], kind[1], cum)


def _proj_heads(x, mods, plan, tt, cum=None):
    n_seq, s, d = x.shape
    vec = pl.BlockSpec((1, 1, d), lambda b, i: (b, 0, 0))
    in_specs = [pl.BlockSpec((1, tt, d), lambda b, i: (b, i, 0))]
    args = [x]
    for g, shift, scale, w in mods:
        in_specs += [pl.BlockSpec((1, d), lambda b, i: (0, 0)), vec, vec,
                     pl.BlockSpec(w.shape, lambda b, i: (0, 0))]
        args += [g.reshape(1, d), shift.reshape(n_seq, 1, d), scale.reshape(n_seq, 1, d), w]
    if cum is not None:
        in_specs.append(pl.BlockSpec((1, tt, HEAD_LANES), lambda b, i: (b, i, 0)))
        args.append(cum)
    out_specs, out_shape = [], []
    for _, _, kind in plan:
        if kind == "nat":
            out_specs.append(pl.BlockSpec((1, tt, d), lambda b, i: (b, i, 0)))
            out_shape.append(jax.ShapeDtypeStruct((n_seq, s, d), F32))
        else:
            out_specs.append(pl.BlockSpec((1, N_HEADS, tt, HEAD_LANES), lambda b, i: (b, 0, i, 0)))
            out_shape.append(jax.ShapeDtypeStruct((n_seq, N_HEADS, s, HEAD_LANES), BF16))
    return pl.pallas_call(
        functools.partial(_projheads_kernel, n_mod=len(mods), use_cum=cum is not None, plan=tuple(plan)),
        grid=(n_seq, s // tt),
        in_specs=in_specs,
        out_specs=out_specs,
        out_shape=out_shape,
        compiler_params=_cparams(("parallel", "parallel")),
        name="proj_heads",
    )(*args)


def _to_heads(x, tt, scale=1.0, mode="plain", cum=None):
    n_seq, s, d = x.shape
    in_specs = [pl.BlockSpec((1, tt, d), lambda b, i: (b, i, 0))]
    args = [x]
    if mode in ("q_aug", "k_aug"):
        in_specs.append(pl.BlockSpec((1, tt, HEAD_LANES), lambda b, i: (b, i, 0)))
        args.append(cum)
    return pl.pallas_call(
        functools.partial(_toheads_kernel, scale=scale, mode=mode),
        grid=(n_seq, s // tt),
        in_specs=in_specs,
        out_specs=pl.BlockSpec((1, N_HEADS, tt, HEAD_LANES), lambda b, i: (b, 0, i, 0)),
        out_shape=jax.ShapeDtypeStruct((n_seq, N_HEADS, s, HEAD_LANES), BF16),
        compiler_params=_cparams(("parallel", "parallel")),
        name="to_heads",
    )(*args)


def _relbias_kernel(rb_ref, o_ref, *, n_q, n_k, split, band_mask):
    h = pl.program_id(0)
    lanes = o_ref.shape[2]
    idx = lax.broadcasted_iota(jnp.int32, (8, lanes), 1)
    dd = jnp.where(idx > split, BAND + (lanes - idx), BAND - idx)
    v = jnp.clip(dd, -REL_CLIP, REL_CLIP) + REL_CLIP

    def fill(k, t):
        return jnp.where(v == k, rb_ref[h, k], t)

    t = lax.fori_loop(0, 2 * REL_CLIP + 1, fill, jnp.zeros((8, lanes), F32))
    x = jnp.broadcast_to(t[0:1, :], (n_q, lanes))
    row = lax.broadcasted_iota(jnp.int32, (n_q, lanes), 0)
    bit = 1
    while bit < n_q:
        x = jnp.where((row & bit) != 0, pltpu.roll(x, bit, 1), x)
        bit *= 2
    if band_mask:
        col = lax.broadcasted_iota(jnp.int32, (n_q, lanes), 1)
        dc = BAND_CHUNKS + (row >> 6) - (col >> 6)
        x = jnp.where((dc >= 0) & (dc <= BAND_CHUNKS), x, NEG_INF)
    o_ref[0] = x


def _relbias_table(rel_bias, n_q, n_k, split, band_mask):
    lanes = -(-n_k // 128) * 128
    return pl.pallas_call(
        functools.partial(_relbias_kernel, n_q=n_q, n_k=n_k, split=split, band_mask=band_mask),
        grid=(N_HEADS,),
        in_specs=[pl.BlockSpec(memory_space=pltpu.SMEM)],
        out_specs=pl.BlockSpec((1, n_q, lanes), lambda h: (h, 0, 0)),
        out_shape=jax.ShapeDtypeStruct((N_HEADS, n_q, lanes), F32),
        compiler_params=_cparams(("arbitrary",)),
        name="relbias_table",
    )(rel_bias)


def _band_kernel(*refs, n_seg, first_seg_is_pad_at_zero):
    q_ref = refs[0]
    k_refs = refs[1:1 + n_seg]
    v_refs = refs[1 + n_seg:1 + 2 * n_seg]
    bias_ref = refs[1 + 2 * n_seg]
    o_ref = refs[2 + 2 * n_seg]
    for j in range(HEADS_PER_STEP):
        q = q_ref[0, j]
        scores = []
        off = 0
        for i in range(n_seg):
            k = k_refs[i][0, j]
            nk = k.shape[0]
            s = _dot_nt(q, k) + bias_ref[j, :, off:off + nk]
            if i == 0 and first_seg_is_pad_at_zero:
                s = jnp.where(pl.program_id(2) > 0, s, NEG_INF)
            scores.append(s)
            off += nk
        m = functools.reduce(jnp.maximum, [jnp.max(s, axis=-1, keepdims=True) for s in scores])
        acc = functools.reduce(jnp.add, [_dot(jnp.exp(s - m).astype(BF16), v_ref[0, j])
                                         for s, v_ref in zip(scores, v_refs)])
        o_ref[0, j] = (acc / acc[:, HEAD_DIM:HEAD_DIM + 1]).astype(BF16)


def _band_attend_prompt(qh, kh, vh, bias):
    n_seq, nh, s, hl = qh.shape
    tg = BAND
    blk = (1, HEADS_PER_STEP, tg, hl)
    cur = lambda h, b, g: (b, h, g, 0)
    prev = lambda h, b, g: (b, h, jnp.maximum(g - 1, 0), 0)
    return pl.pallas_call(
        functools.partial(_band_kernel, n_seg=2, first_seg_is_pad_at_zero=True),
        grid=(nh // HEADS_PER_STEP, n_seq, s // tg),
        in_specs=[pl.BlockSpec(blk, cur),
                  pl.BlockSpec(blk, prev), pl.BlockSpec(blk, cur),
                  pl.BlockSpec(blk, prev), pl.BlockSpec(blk, cur),
                  pl.BlockSpec((HEADS_PER_STEP, tg, 2 * tg), lambda h, b, g: (h, 0, 0))],
        out_specs=pl.BlockSpec(blk, cur),
        out_shape=jax.ShapeDtypeStruct(qh.shape, BF16),
        compiler_params=_cparams(("arbitrary", "arbitrary", "arbitrary")),
        name="band_attend_prompt",
    )(qh, kh, kh, vh, vh, bias)


def _band_attend_step(qh, kh, vh, bias):
    n_seq, nh, sq, hl = qh.shape
    sk = kh.shape[2]
    hps = HEADS_PER_STEP
    return pl.pallas_call(
        functools.partial(_band_kernel, n_seg=1, first_seg_is_pad_at_zero=False),
        grid=(nh // hps, n_seq, 1),
        in_specs=[pl.BlockSpec((1, hps, sq, hl), lambda h, b, g: (b, h, 0, 0)),
                  pl.BlockSpec((1, hps, sk, hl), lambda h, b, g: (b, h, 0, 0)),
                  pl.BlockSpec((1, hps, sk, hl), lambda h, b, g: (b, h, 0, 0)),
                  pl.BlockSpec((hps, sq, bias.shape[2]), lambda h, b, g: (h, 0, 0))],
        out_specs=pl.BlockSpec((1, hps, sq, hl), lambda h, b, g: (b, h, 0, 0)),
        out_shape=jax.ShapeDtypeStruct(qh.shape, BF16),
        compiler_params=_cparams(("arbitrary", "arbitrary", "arbitrary")),
        name="band_attend_step",
    )(qh, kh, vh, bias)


def _fox_kernel(q_ref, k_ref, v_ref, o_ref, *, tq, tk, past):
    qi = pl.program_id(2)
    heads = range(FOX_HEADS_PER_STEP)
    qs = [q_ref[0, j] for j in heads]

    def update(carry, s, v):
        m, acc = carry
        m_new = jnp.maximum(m, jnp.max(s, axis=-1, keepdims=True))
        acc = jnp.exp(m - m_new) * acc + _dot(jnp.exp(s - m_new).astype(BF16), v)
        return m_new, acc

    def full_tile(t, carries):
        rows = pl.ds(pl.multiple_of(t * tk, tk), tk)
        return tuple(update(carries[j], _dot_nt(qs[j], k_ref[0, j, rows, :]), v_ref[0, j, rows, :])
                     for j in heads)

    init = tuple((jnp.full((tq, 1), NEG_INF, F32), jnp.zeros((tq, HEAD_LANES), F32)) for _ in heads)
    q_start = past + qi * tq
    carries = lax.fori_loop(0, q_start // tk, full_tile, init)
    rows = pl.ds(pl.multiple_of(q_start, tq), tq)
    row = lax.broadcasted_iota(jnp.int32, (tq, tq), 0)
    col = lax.broadcasted_iota(jnp.int32, (tq, tq), 1)
    for j in heads:
        s = jnp.where(col <= row, _dot_nt(qs[j], k_ref[0, j, rows, :]), NEG_INF)
        _, acc = update(carries[j], s, v_ref[0, j, rows, :])
        o_ref[0, j] = (acc / acc[:, HEAD_DIM:HEAD_DIM + 1]).astype(BF16)


def _fox_attend(qh, kh, vh, tq, tk, past):
    n_seq, nh, sq, hl = qh.shape
    sk = kh.shape[2]
    hps = FOX_HEADS_PER_STEP
    return pl.pallas_call(
        functools.partial(_fox_kernel, tq=tq, tk=tk, past=past),
        grid=(n_seq, nh // hps, sq // tq),
        in_specs=[pl.BlockSpec((1, hps, tq, hl), lambda b, h, i: (b, h, i, 0)),
                  pl.BlockSpec((1, hps, sk, hl), lambda b, h, i: (b, h, 0, 0)),
                  pl.BlockSpec((1, hps, sk, hl), lambda b, h, i: (b, h, 0, 0))],
        out_specs=pl.BlockSpec((1, hps, tq, hl), lambda b, h, i: (b, h, i, 0)),
        out_shape=jax.ShapeDtypeStruct(qh.shape, BF16),
        compiler_params=_cparams(("arbitrary", "arbitrary", "arbitrary")),
        name="fox_attend",
    )(qh, kh, vh)


def _cumsum_kernel(x_ref, o_ref, *, bs):
    n_blk = x_ref.shape[1] // bs
    r = lax.broadcasted_iota(jnp.int32, (bs, bs), 0)
    c = lax.broadcasted_iota(jnp.int32, (bs, bs), 1)
    tri = jnp.where(c <= r, 1.0, 0.0).astype(BF16)

    def body(i, carry):
        start = pl.multiple_of(i * bs, bs)
        hi, mid, lo = _split3(x_ref[0, pl.ds(start, bs), :])
        out = (_dot(tri, hi) + _dot(tri, mid) + _dot(tri, lo)) + carry
        o_ref[0, pl.ds(start, bs), :] = out
        return out[bs - 1:bs, :]

    lax.fori_loop(0, n_blk, body, jnp.zeros((1, x_ref.shape[2]), F32))


def _cumsum_seq(x, bs):
    n_seq, s, w = x.shape
    return pl.pallas_call(
        functools.partial(_cumsum_kernel, bs=bs),
        grid=(n_seq,),
        in_specs=[pl.BlockSpec((1, s, w), lambda b: (b, 0, 0))],
        out_specs=pl.BlockSpec((1, s, w), lambda b: (b, 0, 0)),
        out_shape=jax.ShapeDtypeStruct(x.shape, F32),
        compiler_params=_cparams(("parallel",)),
        name="cumsum_seq",
    )(x)


def _oproj_kernel(o_ref, w_ref, x_ref, gate_ref, g_ref, sh_ref, sc_ref, out_ref, h_ref):
    acc = _dot(o_ref[0, 0], w_ref[0])
    for h in range(1, N_HEADS):
        acc = acc + _dot(o_ref[0, h], w_ref[h])
    x = x_ref[0] + gate_ref[0] * acc
    out_ref[0] = x
    h_ref[0] = _mod_norm(x, g_ref[...], sh_ref[0], sc_ref[0]).astype(BF16)


def _oproj_residual(oh, w_heads, x, gate, g, shift, scale, tt):
    n_seq, s, d = x.shape
    vec = pl.BlockSpec((1, 1, d), lambda b, i: (b, 0, 0))
    blk = pl.BlockSpec((1, tt, d), lambda b, i: (b, i, 0))
    return pl.pallas_call(
        _oproj_kernel,
        grid=(n_seq, s // tt),
        in_specs=[pl.BlockSpec((1, N_HEADS, tt, HEAD_LANES), lambda b, i: (b, 0, i, 0)),
                  pl.BlockSpec((N_HEADS, HEAD_LANES, d), lambda b, i: (0, 0, 0)),
                  blk, vec, pl.BlockSpec((1, d), lambda b, i: (0, 0)), vec, vec],
        out_specs=[blk, blk],
        out_shape=[jax.ShapeDtypeStruct(x.shape, F32), jax.ShapeDtypeStruct(x.shape, BF16)],
        compiler_params=_cparams(("parallel", "parallel")),
        name="oproj_residual",
    )(oh, w_heads, x, gate.reshape(n_seq, 1, d), g.reshape(1, d),
      shift.reshape(n_seq, 1, d), scale.reshape(n_seq, 1, d))


LANE_CHUNK = 128
SUB_ROWS = 8
PACK_ROWS = 16
RETRIEVE_UNROLL = 4


def _topk_rows(s, row, exact):
    rank = jnp.full(s.shape, float(PEER_TOPK), F32)
    tops = []
    for r in range(PEER_TOPK):
        m = jnp.max(s, axis=0, keepdims=True)
        sel = s == m
        if exact:
            first = jnp.min(jnp.where(sel, row, float(N_KEYS)), axis=0, keepdims=True)
            sel = row == first
        rank = jnp.where(sel, float(r), rank)
        s = jnp.where(sel, -jnp.inf, s)
        tops.append(m)
    count = jnp.sum(jnp.where(rank < float(PEER_TOPK), 1.0, 0.0), axis=0, keepdims=True)
    return rank, tops, count


def _stack16(rows, row16):
    out = jnp.zeros(row16.shape, F32)
    for b, v in enumerate(rows):
        out = jnp.where(row16 == float(b), v, out)
    return out


def _select_pairs(top1, top2, row8, exact):
    lanes = top1[0].shape
    row16 = jnp.concatenate([row8, row8 + 8.0], axis=0)
    t2 = _stack16(top2, row16)
    t2_lo, t2_hi = t2[0:8], t2[8:16]
    cands, poss = [], []
    for a in range(PEER_TOPK):
        nb = PEER_TOPK // (a + 1)
        halves = [(t2_lo, 0)] + ([(t2_hi, 8)] if nb > 8 else [])
        for t2h, b0 in halves:
            c = top1[a] + t2h
            valid = (row8 + float(b0)) < float(nb)
            cands.append(jnp.where(valid, c, -jnp.inf))
            poss.append(jnp.where(valid, row8 + float(a * PEER_TOPK + b0), 1e9))
    orig = list(cands)
    picked = [jnp.zeros(c.shape, F32) for c in cands]
    for _ in range(PEER_TOPK):
        m = jnp.max(functools.reduce(jnp.maximum, cands), axis=0, keepdims=True)
        sels = [c == m for c in cands]
        if exact:
            hit = [jnp.where(sl, p, 1e9) for sl, p in zip(sels, poss)]
            first = jnp.min(functools.reduce(jnp.minimum, hit), axis=0, keepdims=True)
            sels = [p == first for p in poss]
        cands = [jnp.where(sl, -jnp.inf, c) for sl, c in zip(sels, cands)]
        picked = [jnp.where(sl, 1.0, pk) for sl, pk in zip(sels, picked)]
    m0 = top1[0] + top2[0]
    z = jnp.zeros(lanes, F32)
    count = jnp.zeros(lanes, F32)
    n_sel = []
    k = 0
    for a in range(PEER_TOPK):
        nb = PEER_TOPK // (a + 1)
        n_a = jnp.zeros(lanes, F32)
        for _ in range(2 if nb > 8 else 1):
            n_a = n_a + jnp.sum(picked[k], axis=0, keepdims=True)
            e = jnp.where(picked[k] > 0.5, jnp.exp(orig[k] - m0), 0.0)
            z = z + jnp.sum(e, axis=0, keepdims=True)
            k += 1
        n_sel.append(n_a)
        count = count + n_a
    return n_sel, z, count


def _gelu2(x):
    c0 = 0.7978845608028654
    return x + x * jnp.tanh(x * (c0 + (c0 * 0.044715) * (x * x)))


def _peer_kernel(h_ref, wq_ref, keys_ref, u_ref, vt_ref, out_ref,
                 xt_s, acc_s, ht_s, at_s, sc_s, rf_s, e1_s, n_s, *, tt, eb):
    e = pl.program_id(1)
    n_chunks = tt // LANE_CHUNK
    rows_per_blk = eb // N_KEYS

    @pl.when(e == 0)
    def _retrieve():
        xt = h_ref[...].astype(F32).T.astype(BF16)
        xt_s[...] = xt
        acc_s[...] = jnp.zeros_like(acc_s)
        qt = _dot(wq_ref[...], xt)
        for hp in range(2 * PEER_HEADS):
            sc_s[hp] = _dot(keys_ref[hp], qt[hp * N_KEYS:(hp + 1) * N_KEYS].astype(BF16))

        def select(hc, exact):
            h = hc // n_chunks
            c0 = pl.multiple_of((hc % n_chunks) * LANE_CHUNK, LANE_CHUNK)
            lanes = pl.ds(c0, LANE_CHUNK)
            row = lax.broadcasted_iota(jnp.int32, (N_KEYS, LANE_CHUNK), 0).astype(F32)
            row8 = lax.broadcasted_iota(jnp.int32, (8, LANE_CHUNK), 0).astype(F32)
            s1 = sc_s[2 * h, :, lanes]
            s2 = sc_s[2 * h + 1, :, lanes]
            rank1, top1, cnt1 = _topk_rows(s1, row, exact)
            rank2, top2, cnt2 = _topk_rows(s2, row, exact)
            n_sel, z, cnt3 = _select_pairs(top1, top2, row8, exact)
            n_dense = jnp.zeros((N_KEYS, LANE_CHUNK), F32)
            for a in range(PEER_TOPK):
                n_dense = jnp.where(rank1 == float(a), n_sel[a], n_dense)
            n_s[h, :, lanes] = n_dense
            e1_s[h, :, lanes] = jnp.exp(s1 - top1[0])
            f = jnp.exp(s2 - top2[0]) * (0.5 / z)
            for rt in range(N_KEYS // SUB_ROWS):
                keys = slice(rt * SUB_ROWS, (rt + 1) * SUB_ROWS)
                rf_s[hc % n_chunks, rt, 2 * h] = rank2[keys]
                rf_s[hc % n_chunks, rt, 2 * h + 1] = f[keys]
            return jnp.max(jnp.maximum(jnp.maximum(cnt1, cnt2), cnt3))

        def per_head_pair(i, _):
            units = [RETRIEVE_UNROLL * i + k for k in range(RETRIEVE_UNROLL)]
            most = [select(hc, exact=False) for hc in units]
            for hc, m in zip(units, most):
                @pl.when(m > PEER_TOPK + 0.5)
                def _ties():
                    select(hc, exact=True)
            return 0

        lax.fori_loop(0, PEER_HEADS * n_chunks // RETRIEVE_UNROLL, per_head_pair, 0)

    ht_s[...] = _dot(u_ref[...], xt_s[...])

    def per_chunk(c, _):
        lanes = pl.ds(pl.multiple_of(c * LANE_CHUNK, LANE_CHUNK), LANE_CHUNK)
        grp = pl.ds(pl.multiple_of(e * rows_per_blk, rows_per_blk), rows_per_blk)
        n_grp = [n_s[h, grp, lanes] for h in range(PEER_HEADS)]
        e_grp = [e1_s[h, grp, lanes] for h in range(PEER_HEADS)]
        tile = (SUB_ROWS, LANE_CHUNK)
        for il in range(rows_per_blk):
            n8 = [jnp.broadcast_to(n_grp[h][il:il + 1], tile) for h in range(PEER_HEADS)]
            e8 = [jnp.broadcast_to(e_grp[h][il:il + 1], tile) for h in range(PEER_HEADS)]
            for rp in range(N_KEYS // PACK_ROWS):
                halves = []
                for rt in (2 * rp, 2 * rp + 1):
                    w = None
                    for h in range(PEER_HEADS):
                        t = jnp.where(rf_s[c, rt, 2 * h] < n8[h], rf_s[c, rt, 2 * h + 1], 0.0) * e8[h]
                        w = t if w is None else w + t
                    rows = pl.ds(il * N_KEYS + rt * SUB_ROWS, SUB_ROWS)
                    halves.append(_gelu2(ht_s[rows, lanes]) * w)
                rows = pl.ds(il * N_KEYS + rp * PACK_ROWS, PACK_ROWS)
                at_s[rows, lanes] = jnp.concatenate(halves, axis=0).astype(BF16)
        return 0

    lax.fori_loop(0, n_chunks, per_chunk, 0)
    acc_s[...] += _dot(vt_ref[...], at_s[...])

    @pl.when(e == pl.num_programs(1) - 1)
    def _emit():
        out_ref[...] = acc_s[...].T


def _peer(h_flat, wq_t, keys, u, v_t, tt, eb):
    t, d = h_flat.shape
    n_exp = u.shape[0]
    assert t % tt == 0 and eb == 8 * N_KEYS and n_exp % eb == 0
    scr = pltpu.VMEM
    return pl.pallas_call(
        functools.partial(_peer_kernel, tt=tt, eb=eb),
        grid=(t // tt, n_exp // eb),
        in_specs=[pl.BlockSpec((tt, d), lambda i, e: (i, 0)),
                  pl.BlockSpec(wq_t.shape, lambda i, e: (0, 0)),
                  pl.BlockSpec(keys.shape, lambda i, e: (0, 0, 0)),
                  pl.BlockSpec((eb, d), lambda i, e: (e, 0)),
                  pl.BlockSpec((d, eb), lambda i, e: (0, e))],
        out_specs=pl.BlockSpec((tt, d), lambda i, e: (i, 0)),
        out_shape=jax.ShapeDtypeStruct((t, d), F32),
        scratch_shapes=[scr((d, tt), BF16), scr((d, tt), F32), scr((eb, tt), F32), scr((eb, tt), BF16),
                        scr((2 * PEER_HEADS, N_KEYS, tt), F32),
                        scr((tt // LANE_CHUNK, N_KEYS // SUB_ROWS, 2 * PEER_HEADS, SUB_ROWS, LANE_CHUNK), F32),
                        scr((PEER_HEADS, N_KEYS, tt), F32), scr((PEER_HEADS, N_KEYS, tt), F32)],
        compiler_params=_cparams(("parallel", "arbitrary")),
        name="peer_dense",
    )(h_flat, wq_t, keys, u, v_t)


def _residual_kernel(x_ref, gate_ref, o_ref, *rest, final):
    x = x_ref[0] + gate_ref[0] * o_ref[0]
    if final:
        g_ref, out_ref = rest
        out_ref[0] = (x * lax.rsqrt(jnp.mean(x * x, axis=-1, keepdims=True) + EPS)) * g_ref[...]
    else:
        (out_ref,) = rest
        out_ref[0] = x


def _residual(x, gate, o, tt, g_final=None):
    n_seq, s, d = x.shape
    final = g_final is not None
    blk = pl.BlockSpec((1, tt, d), lambda b, i: (b, i, 0))
    in_specs = [blk, pl.BlockSpec((1, 1, d), lambda b, i: (b, 0, 0)), blk]
    args = [x, gate.reshape(n_seq, 1, d), o]
    if final:
        in_specs.append(pl.BlockSpec((1, d), lambda b, i: (0, 0)))
        args.append(g_final.reshape(1, d))
    return pl.pallas_call(
        functools.partial(_residual_kernel, final=final),
        grid=(n_seq, s // tt),
        in_specs=in_specs,
        out_specs=blk,
        out_shape=jax.ShapeDtypeStruct(x.shape, F32),
        compiler_params=_cparams(("parallel", "parallel")),
        name="residual",
    )(*args)


def _row_tile(s):
    return 256 if s % 256 == 0 else s


def _peer_tile(t):
    return 512 if t % 512 == 0 else 256


def _pad_lanes(x, n):
    return jnp.pad(x, [(0, 0)] * (x.ndim - 1) + [(0, n - x.shape[-1])])


def _prepare_weights(w_qkv_a, w_o_a, w_kv, w_f, b_f, w_q_b, w_o_b, w_pq, sub_keys, peer_u, peer_v):
    depth = w_pq.shape[0]

    def o_heads(w):
        w = w.reshape(N_HEADS, HEAD_DIM, D_MODEL)
        return jnp.pad(w, ((0, 0), (0, HEAD_LANES - HEAD_DIM), (0, 0))).astype(BF16)

    return dict(
        w_qkv=w_qkv_a[0].astype(BF16),
        w_o_a=o_heads(w_o_a[0]),
        w_kv=w_kv.astype(BF16),
        w_f=_pad_lanes(w_f, HEAD_LANES).astype(BF16),
        b_f=_pad_lanes(b_f, HEAD_LANES),
        w_q_b=w_q_b[0].astype(BF16),
        w_o_b=o_heads(w_o_b[0]),
        w_pq_t=[w_pq[l].T.astype(BF16) for l in range(depth)],
        keys=[jnp.swapaxes(sub_keys[l], 0, 1).reshape(2 * PEER_HEADS, N_KEYS, -1).astype(BF16)
              for l in range(depth)],
        u=[peer_u[l].astype(BF16) for l in range(depth)],
        v_t=[peer_v[l].T.astype(BF16) for l in range(depth)],
    )


def _peer_layer(x, h, gate, wts, l, g_final=None):
    n_seq, s, d = x.shape
    o = _peer(h.reshape(n_seq * s, d), wts["w_pq_t"][l], wts["keys"][l], wts["u"][l], wts["v_t"][l],
              tt=_peer_tile(n_seq * s), eb=8 * N_KEYS)
    return _residual(x, gate, o.reshape(n_seq, s, d), _row_tile(s), g_final=g_final)


def _with_cache(cache_heads, new_heads):
    return new_heads if cache_heads is None else jnp.concatenate([cache_heads, new_heads], axis=2)


def _trunk(x, mods0, mods1, modkv, caches, wts, g_norm, g_kv, g_final, rel_bias):
    n_seq, s, d = x.shape
    tt = _row_tile(s)
    sh_m, sc_m, gt_m, sh_c, sc_c, gt_c = jnp.split(mods0, 6, axis=-1)

    q_scale = HEAD_DIM ** -0.5
    heads = lambda scale, mode: (scale, mode)

    qh, kh, vh, k, v = _proj_heads(
        x, [(g_norm[0, 0], sh_m, sc_m, wts["w_qkv"])],
        [(0, 0, heads(q_scale, "plain")), (0, d, heads(1.0, "plain")), (0, 2 * d, heads(1.0, "ones")),
         (0, d, "nat"), (0, 2 * d, "nat")], tt)
    if caches is None:
        bias = _relbias_table(rel_bias, BAND, 2 * BAND, split=768, band_mask=True)
        oh = _band_attend_prompt(qh, kh, vh, bias)
        keep = min(BAND, s)
        new_ak, new_av = k[:, s - keep:], v[:, s - keep:]
    else:
        cache_a_k, cache_a_v = caches[0], caches[1]
        win = cache_a_k.shape[2]
        kh = _with_cache(_to_heads(cache_a_k[0].reshape(n_seq, win, d), _row_tile(win)), kh)
        vh = _with_cache(_to_heads(cache_a_v[0].reshape(n_seq, win, d), _row_tile(win), mode="ones"), vh)
        lanes = -(-(win + s) // 128) * 128
        bias = _relbias_table(rel_bias, s, win + s, split=(win + s + lanes - s) // 2, band_mask=False)
        oh = _band_attend_step(qh, kh, vh, bias)
        new_ak, new_av = k, v
    x, h = _oproj_residual(oh, wts["w_o_a"], x, gt_m, g_norm[0, 1], sh_c, sc_c, tt)
    x = _peer_layer(x, h, gt_c, wts, 0)

    sh_m, sc_m, gt_m, sh_c, sc_c, gt_c = jnp.split(mods1, 6, axis=-1)
    sh_kv, sc_kv = jnp.split(modkv, 2, axis=-1)
    (lf,) = _norm_proj(x, g_kv, sh_kv, sc_kv, wts["w_f"], (HEAD_LANES,), tt, bias_last=wts["b_f"])
    if caches is None:
        past = 0
        lf_all = lf
        tq = tk = 512 if s % 512 == 0 else s
        bs = 256 if s % 256 == 0 else s
    else:
        cache_b_k, cache_b_v, cache_b_logf = caches[2], caches[3], caches[4]
        past = cache_b_k.shape[1]
        lf_all = jnp.concatenate([_pad_lanes(cache_b_logf.astype(F32), HEAD_LANES), lf], axis=1)
        tq, tk = s, 512
        bs = s
    cum = _cumsum_seq(lf_all, bs)
    qh, kh, vh, bk, bv = _proj_heads(
        x, [(g_kv, sh_kv, sc_kv, wts["w_kv"]), (g_norm[1, 0], sh_m, sc_m, wts["w_q_b"])],
        [(1, 0, heads(q_scale, "q_aug")), (0, 0, heads(1.0, "k_aug")), (0, d, heads(1.0, "ones")),
         (0, 0, "nat"), (0, d, "nat")], tt, cum=cum[:, past:])
    if caches is not None:
        kh = _with_cache(_to_heads(cache_b_k.reshape(n_seq, past, d), _row_tile(past),
                                   mode="k_aug", cum=cum[:, :past]), kh)
        vh = _with_cache(_to_heads(cache_b_v.reshape(n_seq, past, d), _row_tile(past), mode="ones"), vh)
    oh = _fox_attend(qh, kh, vh, tq, tk, past)
    x, h = _oproj_residual(oh, wts["w_o_b"], x, gt_m, g_norm[1, 1], sh_c, sc_c, tt)
    y = _peer_layer(x, h, gt_c, wts, 1, g_final=g_final)

    hshape = (n_seq, -1, N_HEADS, HEAD_DIM)
    return (y, new_ak.reshape(hshape)[None], new_av.reshape(hshape)[None],
            bk.reshape(hshape), bv.reshape(hshape), lf[:, :, :N_HEADS])


def kernel(x_prompt, x_sample, c_prompt, c_sample, cache_a_k, cache_a_v, cache_b_k, cache_b_v, cache_b_logf, w_ada, b_ada, g_norm, w_qkv_a, rel_bias_a, w_o_a, w_ada_kv, b_ada_kv, g_kv, w_kv, w_f, b_f, w_q_b, w_o_b, w_pq, sub_keys, peer_u, peer_v, g_final):
    n_p, n_s = c_prompt.shape[0], c_sample.shape[0]
    n_c = -(-(n_p + n_s) // 16) * 16
    c_all = jnp.pad(jnp.concatenate([c_prompt, c_sample], axis=0), ((0, n_c - n_p - n_s), (0, 0)))
    mods0 = _cond_matmul(c_all, w_ada[0], b_ada[0])
    mods1 = _cond_matmul(c_all, w_ada[1], b_ada[1])
    modkv = _cond_matmul(c_all, w_ada_kv, b_ada_kv)
    wts = _prepare_weights(w_qkv_a, w_o_a, w_kv, w_f, b_f, w_q_b, w_o_b, w_pq, sub_keys, peer_u, peer_v)

    out_p = _trunk(x_prompt, mods0[:n_p], mods1[:n_p], modkv[:n_p], None,
                   wts, g_norm, g_kv, g_final, rel_bias_a[0])
    out_s = _trunk(x_sample, mods0[n_p:n_p + n_s], mods1[n_p:n_p + n_s], modkv[n_p:n_p + n_s],
                   (cache_a_k, cache_a_v, cache_b_k, cache_b_v, cache_b_logf),
                   wts, g_norm, g_kv, g_final, rel_bias_a[0])
    return (out_p[0], out_s[0]) + out_p[1:] + out_s[1:]
```

```python
import functools

import jax
import jax.numpy as jnp
from jax import lax
from jax.experimental import pallas as pl
from jax.experimental.pallas import tpu as pltpu

F32 = jnp.float32
BF16 = jnp.bfloat16

D_MODEL = 1024
HEAD_DIM = 64
N_HEADS = D_MODEL // HEAD_DIM
HEAD_LANES = 128
HEADS_PER_STEP = 8
FOX_HEADS_PER_STEP = 4
CHUNK = 64
BAND_CHUNKS = 8
BAND = BAND_CHUNKS * CHUNK
REL_CLIP = 128
PEER_HEADS = 8
N_KEYS = 128
PEER_TOPK = 16
EPS = 1e-6
NEG_INF = -1e30

VMEM_LIMIT = 56 * 1024 * 1024


def _cparams(sem, vmem=VMEM_LIMIT):
    return pltpu.CompilerParams(dimension_semantics=sem, vmem_limit_bytes=vmem)


def _dot(a, b):
    return jnp.dot(a, b, preferred_element_type=F32)


def _dot_nt(a, b):
    return lax.dot_general(a, b, (((1,), (1,)), ((), ())), preferred_element_type=F32)


def _split3(x):
    hi = x.astype(BF16)
    r = x - hi.astype(F32)
    mid = r.astype(BF16)
    lo = (r - mid.astype(F32)).astype(BF16)
    return hi, mid, lo


def _cond_kernel(c_ref, w_ref, b_ref, o_ref):
    ch, cm, cl = _split3(c_ref[...])
    wh, wm, wl = _split3(w_ref[...])
    acc = _dot(ch, wh) + (_dot(ch, wm) + _dot(cm, wh)) + (_dot(ch, wl) + _dot(cl, wh) + _dot(cm, wm))
    o_ref[...] = acc + b_ref[...]


def _cond_matmul(c, w, b):
    m, d = c.shape
    n = w.shape[1]
    bn = 1024
    return pl.pallas_call(
        _cond_kernel,
        grid=(n // bn,),
        in_specs=[pl.BlockSpec((m, d), lambda j: (0, 0)),
                  pl.BlockSpec((d, bn), lambda j: (0, j)),
                  pl.BlockSpec((1, bn), lambda j: (0, j))],
        out_specs=pl.BlockSpec((m, bn), lambda j: (0, j)),
        out_shape=jax.ShapeDtypeStruct((m, n), F32),
        compiler_params=_cparams(("arbitrary",)),
        name="cond_matmul",
    )(c, w, b.reshape(1, n))


def _mod_norm(x, g, shift, scale):
    y = x * lax.rsqrt(jnp.mean(x * x, axis=-1, keepdims=True) + EPS)
    return (y * g) * (1.0 + scale) + shift


def _log_sigmoid(x):
    return jnp.minimum(x, 0.0) - jnp.log1p(jnp.exp(-jnp.abs(x)))


def _normproj_kernel(x_ref, g_ref, sh_ref, sc_ref, w_ref, *rest, splits, logsig_last):
    if logsig_last:
        bl_ref, o_refs = rest[0], rest[1:]
    else:
        o_refs = rest
    h = _mod_norm(x_ref[0], g_ref[...], sh_ref[0], sc_ref[0])
    out = _dot(h.astype(BF16), w_ref[...])
    off = 0
    for k, (o_ref, n) in enumerate(zip(o_refs, splits)):
        piece = out[:, off:off + n]
        if logsig_last and k == len(splits) - 1:
            piece = _log_sigmoid(piece + bl_ref[...])
        o_ref[0] = piece
        off += n


def _norm_proj(x, g, shift, scale, w, splits, tt, bias_last=None):
    n_seq, s, d = x.shape
    n = w.shape[1]
    logsig_last = bias_last is not None
    vec = pl.BlockSpec((1, 1, d), lambda b, i: (b, 0, 0))
    in_specs = [pl.BlockSpec((1, tt, d), lambda b, i: (b, i, 0)),
                pl.BlockSpec((1, d), lambda b, i: (0, 0)), vec, vec,
                pl.BlockSpec((d, n), lambda b, i: (0, 0))]
    args = [x, g.reshape(1, d), shift.reshape(n_seq, 1, d), scale.reshape(n_seq, 1, d), w]
    if logsig_last:
        in_specs.append(pl.BlockSpec((1, splits[-1]), lambda b, i: (0, 0)))
        args.append(bias_last.reshape(1, splits[-1]))
    return pl.pallas_call(
        functools.partial(_normproj_kernel, splits=tuple(splits), logsig_last=logsig_last),
        grid=(n_seq, s // tt),
        in_specs=in_specs,
        out_specs=[pl.BlockSpec((1, tt, k), lambda b, i: (b, i, 0)) for k in splits],
        out_shape=[jax.ShapeDtypeStruct((n_seq, s, k), F32) for k in splits],
        compiler_params=_cparams(("parallel", "parallel")),
        name="norm_proj",
    )(*args)


def _write_heads(x, o_ref, scale, mode, cum):
    tt = x.shape[0]
    lane = lax.broadcasted_iota(jnp.int32, (tt, HEAD_LANES), 1)
    for pair in range(N_HEADS // 2):
        col = x[:, pair * HEAD_LANES:(pair + 1) * HEAD_LANES]
        for par in range(2):
            h = 2 * pair + par
            c = col if par == 0 else pltpu.roll(col, HEAD_DIM, 1)
            if scale != 1.0:
                c = c * scale
            if mode == "plain":
                aug = jnp.zeros_like(c)
            elif mode == "ones":
                aug = jnp.where(lane == HEAD_DIM, 1.0, 0.0)
            else:
                cc = cum[:, h:h + 1]
                c1 = cc.astype(BF16).astype(F32)
                r = cc - c1
                c2 = r.astype(BF16).astype(F32)
                c3 = r - c2
                one = jnp.ones_like(c)
                if mode == "q_aug":
                    aug = jnp.where(lane == 64, c1, jnp.where(lane == 65, c2, jnp.where(lane == 66, c3, one)))
                else:
                    aug = jnp.where(lane == 67, -c1, jnp.where(lane == 68, -c2, jnp.where(lane == 69, -c3, one)))
                aug = jnp.where(lane < 70, aug, 0.0)
            o_ref[0, h] = jnp.where(lane < HEAD_DIM, c, aug).astype(BF16)


def _toheads_kernel(x_ref, *rest, scale, mode):
    if mode in ("q_aug", "k_aug"):
        cum_ref, o_ref = rest
        cum = cum_ref[0]
    else:
        (o_ref,) = rest
        cum = None
    _write_heads(x_ref[0], o_ref, scale, mode, cum)


def _projheads_kernel(*refs, n_mod, use_cum, plan):
    x_ref = refs[0]
    mods = [refs[1 + 4 * i:5 + 4 * i] for i in range(n_mod)]
    pos = 1 + 4 * n_mod
    cum = refs[pos][0] if use_cum else None
    out_refs = refs[pos + use_cum:]
    x = x_ref[0]
    y = x * lax.rsqrt(jnp.mean(x * x, axis=-1, keepdims=True) + EPS)
    outs = []
    for g_ref, sh_ref, sc_ref, w_ref in mods:
        h = (y * g_ref[...]) * (1.0 + sc_ref[0]) + sh_ref[0]
        outs.append(_dot(h.astype(BF16), w_ref[...]))
    for (m, off, kind), o_ref in zip(plan, out_refs):
        blk = outs[m][:, off:off + D_MODEL]
        if kind == "nat":
            o_ref[0] = blk
        else:
            _write_heads(blk, o_ref, kind[0], kind[1], cum)


def _proj_heads(x, mods, plan, tt, cum=None):
    n_seq, s, d = x.shape
    vec = pl.BlockSpec((1, 1, d), lambda b, i: (b, 0, 0))
    in_specs = [pl.BlockSpec((1, tt, d), lambda b, i: (b, i, 0))]
    args = [x]
    for g, shift, scale, w in mods:
        in_specs += [pl.BlockSpec((1, d), lambda b, i: (0, 0)), vec, vec,
                     pl.BlockSpec(w.shape, lambda b, i: (0, 0))]
        args += [g.reshape(1, d), shift.reshape(n_seq, 1, d), scale.reshape(n_seq, 1, d), w]
    if cum is not None:
        in_specs.append(pl.BlockSpec((1, tt, HEAD_LANES), lambda b, i: (b, i, 0)))
        args.append(cum)
    out_specs, out_shape = [], []
    for _, _, kind in plan:
        if kind == "nat":
            out_specs.append(pl.BlockSpec((1, tt, d), lambda b, i: (b, i, 0)))
            out_shape.append(jax.ShapeDtypeStruct((n_seq, s, d), F32))
        else:
            out_specs.append(pl.BlockSpec((1, N_HEADS, tt, HEAD_LANES), lambda b, i: (b, 0, i, 0)))
            out_shape.append(jax.ShapeDtypeStruct((n_seq, N_HEADS, s, HEAD_LANES), BF16))
    return pl.pallas_call(
        functools.partial(_projheads_kernel, n_mod=len(mods), use_cum=cum is not None, plan=tuple(plan)),
        grid=(n_seq, s // tt),
        in_specs=in_specs,
        out_specs=out_specs,
        out_shape=out_shape,
        compiler_params=_cparams(("parallel", "parallel")),
        name="proj_heads",
    )(*args)


def _to_heads(x, tt, scale=1.0, mode="plain", cum=None):
    n_seq, s, d = x.shape
    in_specs = [pl.BlockSpec((1, tt, d), lambda b, i: (b, i, 0))]
    args = [x]
    if mode in ("q_aug", "k_aug"):
        in_specs.append(pl.BlockSpec((1, tt, HEAD_LANES), lambda b, i: (b, i, 0)))
        args.append(cum)
    return pl.pallas_call(
        functools.partial(_toheads_kernel, scale=scale, mode=mode),
        grid=(n_seq, s // tt),
        in_specs=in_specs,
        out_specs=pl.BlockSpec((1, N_HEADS, tt, HEAD_LANES), lambda b, i: (b, 0, i, 0)),
        out_shape=jax.ShapeDtypeStruct((n_seq, N_HEADS, s, HEAD_LANES), BF16),
        compiler_params=_cparams(("parallel", "parallel")),
        name="to_heads",
    )(*args)


def _relbias_kernel(rb_ref, o_ref, *, n_q, n_k, split, band_mask):
    h = pl.program_id(0)
    lanes = o_ref.shape[2]
    idx = lax.broadcasted_iota(jnp.int32, (8, lanes), 1)
    dd = jnp.where(idx > split, BAND + (lanes - idx), BAND - idx)
    v = jnp.clip(dd, -REL_CLIP, REL_CLIP) + REL_CLIP

    def fill(k, t):
        return jnp.where(v == k, rb_ref[h, k], t)

    t = lax.fori_loop(0, 2 * REL_CLIP + 1, fill, jnp.zeros((8, lanes), F32))
    x = jnp.broadcast_to(t[0:1, :], (n_q, lanes))
    row = lax.broadcasted_iota(jnp.int32, (n_q, lanes), 0)
    bit = 1
    while bit < n_q:
        x = jnp.where((row & bit) != 0, pltpu.roll(x, bit, 1), x)
        bit *= 2
    if band_mask:
        col = lax.broadcasted_iota(jnp.int32, (n_q, lanes), 1)
        dc = BAND_CHUNKS + (row >> 6) - (col >> 6)
        x = jnp.where((dc >= 0) & (dc <= BAND_CHUNKS), x, NEG_INF)
    o_ref[0] = x


def _relbias_table(rel_bias, n_q, n_k, split, band_mask):
    lanes = -(-n_k // 128) * 128
    return pl.pallas_call(
        functools.partial(_relbias_kernel, n_q=n_q, n_k=n_k, split=split, band_mask=band_mask),
        grid=(N_HEADS,),
        in_specs=[pl.BlockSpec(memory_space=pltpu.SMEM)],
        out_specs=pl.BlockSpec((1, n_q, lanes), lambda h: (h, 0, 0)),
        out_shape=jax.ShapeDtypeStruct((N_HEADS, n_q, lanes), F32),
        compiler_params=_cparams(("arbitrary",)),
        name="relbias_table",
    )(rel_bias)


def _band_kernel(*refs, n_seg, first_seg_is_pad_at_zero):
    q_ref = refs[0]
    k_refs = refs[1:1 + n_seg]
    v_refs = refs[1 + n_seg:1 + 2 * n_seg]
    bias_ref = refs[1 + 2 * n_seg]
    o_ref = refs[2 + 2 * n_seg]
    for j in range(HEADS_PER_STEP):
        q = q_ref[0, j]
        scores = []
        off = 0
        for i in range(n_seg):
            k = k_refs[i][0, j]
            nk = k.shape[0]
            s = _dot_nt(q, k) + bias_ref[j, :, off:off + nk]
            if i == 0 and first_seg_is_pad_at_zero:
                s = jnp.where(pl.program_id(2) > 0, s, NEG_INF)
            scores.append(s)
            off += nk
        m = functools.reduce(jnp.maximum, [jnp.max(s, axis=-1, keepdims=True) for s in scores])
        acc = functools.reduce(jnp.add, [_dot(jnp.exp(s - m).astype(BF16), v_ref[0, j])
                                         for s, v_ref in zip(scores, v_refs)])
        o_ref[0, j] = (acc / acc[:, HEAD_DIM:HEAD_DIM + 1]).astype(BF16)


def _band_attend_prompt(qh, kh, vh, bias):
    n_seq, nh, s, hl = qh.shape
    tg = BAND
    blk = (1, HEADS_PER_STEP, tg, hl)
    cur = lambda h, b, g: (b, h, g, 0)
    prev = lambda h, b, g: (b, h, jnp.maximum(g - 1, 0), 0)
    return pl.pallas_call(
        functools.partial(_band_kernel, n_seg=2, first_seg_is_pad_at_zero=True),
        grid=(nh // HEADS_PER_STEP, n_seq, s // tg),
        in_specs=[pl.BlockSpec(blk, cur),
                  pl.BlockSpec(blk, prev), pl.BlockSpec(blk, cur),
                  pl.BlockSpec(blk, prev), pl.BlockSpec(blk, cur),
                  pl.BlockSpec((HEADS_PER_STEP, tg, 2 * tg), lambda h, b, g: (h, 0, 0))],
        out_specs=pl.BlockSpec(blk, cur),
        out_shape=jax.ShapeDtypeStruct(qh.shape, BF16),
        compiler_params=_cparams(("arbitrary", "arbitrary", "arbitrary")),
        name="band_attend_prompt",
    )(qh, kh, kh, vh, vh, bias)


def _band_attend_step(qh, kh, vh, bias):
    n_seq, nh, sq, hl = qh.shape
    sk = kh.shape[2]
    hps = HEADS_PER_STEP
    return pl.pallas_call(
        functools.partial(_band_kernel, n_seg=1, first_seg_is_pad_at_zero=False),
        grid=(nh // hps, n_seq, 1),
        in_specs=[pl.BlockSpec((1, hps, sq, hl), lambda h, b, g: (b, h, 0, 0)),
                  pl.BlockSpec((1, hps, sk, hl), lambda h, b, g: (b, h, 0, 0)),
                  pl.BlockSpec((1, hps, sk, hl), lambda h, b, g: (b, h, 0, 0)),
                  pl.BlockSpec((hps, sq, bias.shape[2]), lambda h, b, g: (h, 0, 0))],
        out_specs=pl.BlockSpec((1, hps, sq, hl), lambda h, b, g: (b, h, 0, 0)),
        out_shape=jax.ShapeDtypeStruct(qh.shape, BF16),
        compiler_params=_cparams(("arbitrary", "arbitrary", "arbitrary")),
        name="band_attend_step",
    )(qh, kh, vh, bias)


def _fox_kernel(q_ref, k_ref, v_ref, o_ref, *, tq, tk, past):
    qi = pl.program_id(2)
    heads = range(FOX_HEADS_PER_STEP)
    qs = [q_ref[0, j] for j in heads]

    def update(carry, s, v):
        m, acc = carry
        m_new = jnp.maximum(m, jnp.max(s, axis=-1, keepdims=True))
        acc = jnp.exp(m - m_new) * acc + _dot(jnp.exp(s - m_new).astype(BF16), v)
        return m_new, acc

    def full_tile(t, carries):
        rows = pl.ds(pl.multiple_of(t * tk, tk), tk)
        return tuple(update(carries[j], _dot_nt(qs[j], k_ref[0, j, rows, :]), v_ref[0, j, rows, :])
                     for j in heads)

    init = tuple((jnp.full((tq, 1), NEG_INF, F32), jnp.zeros((tq, HEAD_LANES), F32)) for _ in heads)
    q_start = past + qi * tq
    carries = lax.fori_loop(0, q_start // tk, full_tile, init)
    rows = pl.ds(pl.multiple_of(q_start, tq), tq)
    row = lax.broadcasted_iota(jnp.int32, (tq, tq), 0)
    col = lax.broadcasted_iota(jnp.int32, (tq, tq), 1)
    for j in heads:
        s = jnp.where(col <= row, _dot_nt(qs[j], k_ref[0, j, rows, :]), NEG_INF)
        _, acc = update(carries[j], s, v_ref[0, j, rows, :])
        o_ref[0, j] = (acc / acc[:, HEAD_DIM:HEAD_DIM + 1]).astype(BF16)


def _fox_attend(qh, kh, vh, tq, tk, past):
    n_seq, nh, sq, hl = qh.shape
    sk = kh.shape[2]
    hps = FOX_HEADS_PER_STEP
    return pl.pallas_call(
        functools.partial(_fox_kernel, tq=tq, tk=tk, past=past),
        grid=(n_seq, nh // hps, sq // tq),
        in_specs=[pl.BlockSpec((1, hps, tq, hl), lambda b, h, i: (b, h, i, 0)),
                  pl.BlockSpec((1, hps, sk, hl), lambda b, h, i: (b, h, 0, 0)),
                  pl.BlockSpec((1, hps, sk, hl), lambda b, h, i: (b, h, 0, 0))],
        out_specs=pl.BlockSpec((1, hps, tq, hl), lambda b, h, i: (b, h, i, 0)),
        out_shape=jax.ShapeDtypeStruct(qh.shape, BF16),
        compiler_params=_cparams(("arbitrary", "arbitrary", "arbitrary")),
        name="fox_attend",
    )(qh, kh, vh)


def _cumsum_kernel(x_ref, o_ref, *, bs):
    n_blk = x_ref.shape[1] // bs
    r = lax.broadcasted_iota(jnp.int32, (bs, bs), 0)
    c = lax.broadcasted_iota(jnp.int32, (bs, bs), 1)
    tri = jnp.where(c <= r, 1.0, 0.0).astype(BF16)

    def body(i, carry):
        start = pl.multiple_of(i * bs, bs)
        hi, mid, lo = _split3(x_ref[0, pl.ds(start, bs), :])
        out = (_dot(tri, hi) + _dot(tri, mid) + _dot(tri, lo)) + carry
        o_ref[0, pl.ds(start, bs), :] = out
        return out[bs - 1:bs, :]

    lax.fori_loop(0, n_blk, body, jnp.zeros((1, x_ref.shape[2]), F32))


def _cumsum_seq(x, bs):
    n_seq, s, w = x.shape
    return pl.pallas_call(
        functools.partial(_cumsum_kernel, bs=bs),
        grid=(n_seq,),
        in_specs=[pl.BlockSpec((1, s, w), lambda b: (b, 0, 0))],
        out_specs=pl.BlockSpec((1, s, w), lambda b: (b, 0, 0)),
        out_shape=jax.ShapeDtypeStruct(x.shape, F32),
        compiler_params=_cparams(("parallel",)),
        name="cumsum_seq",
    )(x)


def _oproj_kernel(o_ref, w_ref, x_ref, gate_ref, g_ref, sh_ref, sc_ref, out_ref, h_ref):
    acc = _dot(o_ref[0, 0], w_ref[0])
    for h in range(1, N_HEADS):
        acc = acc + _dot(o_ref[0, h], w_ref[h])
    x = x_ref[0] + gate_ref[0] * acc
    out_ref[0] = x
    h_ref[0] = _mod_norm(x, g_ref[...], sh_ref[0], sc_ref[0]).astype(BF16)


def _oproj_residual(oh, w_heads, x, gate, g, shift, scale, tt):
    n_seq, s, d = x.shape
    vec = pl.BlockSpec((1, 1, d), lambda b, i: (b, 0, 0))
    blk = pl.BlockSpec((1, tt, d), lambda b, i: (b, i, 0))
    return pl.pallas_call(
        _oproj_kernel,
        grid=(n_seq, s // tt),
        in_specs=[pl.BlockSpec((1, N_HEADS, tt, HEAD_LANES), lambda b, i: (b, 0, i, 0)),
                  pl.BlockSpec((N_HEADS, HEAD_LANES, d), lambda b, i: (0, 0, 0)),
                  blk, vec, pl.BlockSpec((1, d), lambda b, i: (0, 0)), vec, vec],
        out_specs=[blk, blk],
        out_shape=[jax.ShapeDtypeStruct(x.shape, F32), jax.ShapeDtypeStruct(x.shape, BF16)],
        compiler_params=_cparams(("parallel", "parallel")),
        name="oproj_residual",
    )(oh, w_heads, x, gate.reshape(n_seq, 1, d), g.reshape(1, d),
      shift.reshape(n_seq, 1, d), scale.reshape(n_seq, 1, d))


LANE_CHUNK = 128
SUB_ROWS = 8
PACK_ROWS = 16
RETRIEVE_UNROLL = 4


def _topk_rows(s, row, exact):
    rank = jnp.full(s.shape, float(PEER_TOPK), F32)
    tops = []
    for r in range(PEER_TOPK):
        m = jnp.max(s, axis=0, keepdims=True)
        sel = s == m
        if exact:
            first = jnp.min(jnp.where(sel, row, float(N_KEYS)), axis=0, keepdims=True)
            sel = row == first
        rank = jnp.where(sel, float(r), rank)
        s = jnp.where(sel, -jnp.inf, s)
        tops.append(m)
    count = jnp.sum(jnp.where(rank < float(PEER_TOPK), 1.0, 0.0), axis=0, keepdims=True)
    return rank, tops, count


def _stack16(rows, row16):
    out = jnp.zeros(row16.shape, F32)
    for b, v in enumerate(rows):
        out = jnp.where(row16 == float(b), v, out)
    return out


def _select_pairs(top1, top2, row8, exact):
    lanes = top1[0].shape
    row16 = jnp.concatenate([row8, row8 + 8.0], axis=0)
    t2 = _stack16(top2, row16)
    t2_lo, t2_hi = t2[0:8], t2[8:16]
    cands, poss = [], []
    for a in range(PEER_TOPK):
        nb = PEER_TOPK // (a + 1)
        halves = [(t2_lo, 0)] + ([(t2_hi, 8)] if nb > 8 else [])
        for t2h, b0 in halves:
            c = top1[a] + t2h
            valid = (row8 + float(b0)) < float(nb)
            cands.append(jnp.where(valid, c, -jnp.inf))
            poss.append(jnp.where(valid, row8 + float(a * PEER_TOPK + b0), 1e9))
    orig = list(cands)
    picked = [jnp.zeros(c.shape, F32) for c in cands]
    for _ in range(PEER_TOPK):
        m = jnp.max(functools.reduce(jnp.maximum, cands), axis=0, keepdims=True)
        sels = [c == m for c in cands]
        if exact:
            hit = [jnp.where(sl, p, 1e9) for sl, p in zip(sels, poss)]
            first = jnp.min(functools.reduce(jnp.minimum, hit), axis=0, keepdims=True)
            sels = [p == first for p in poss]
        cands = [jnp.where(sl, -jnp.inf, c) for sl, c in zip(sels, cands)]
        picked = [jnp.where(sl, 1.0, pk) for sl, pk in zip(sels, picked)]
    m0 = top1[0] + top2[0]
    z = jnp.zeros(lanes, F32)
    count = jnp.zeros(lanes, F32)
    n_sel = []
    k = 0
    for a in range(PEER_TOPK):
        nb = PEER_TOPK // (a + 1)
        n_a = jnp.zeros(lanes, F32)
        for _ in range(2 if nb > 8 else 1):
            n_a = n_a + jnp.sum(picked[k], axis=0, keepdims=True)
            e = jnp.where(picked[k] > 0.5, jnp.exp(orig[k] - m0), 0.0)
            z = z + jnp.sum(e, axis=0, keepdims=True)
            k += 1
        n_sel.append(n_a)
        count = count + n_a
    return n_sel, z, count


def _gelu2(x):
    c0 = 0.7978845608028654
    return x + x * jnp.tanh(x * (c0 + (c0 * 0.044715) * (x * x)))


def _peer_kernel(h_ref, wq_ref, keys_ref, u_ref, vt_ref, out_ref,
                 xt_s, acc_s, ht_s, at_s, sc_s, rf_s, e1_s, n_s, *, tt, eb):
    e = pl.program_id(1)
    n_chunks = tt // LANE_CHUNK
    rows_per_blk = eb // N_KEYS

    @pl.when(e == 0)
    def _retrieve():
        xt = h_ref[...].astype(F32).T.astype(BF16)
        xt_s[...] = xt
        acc_s[...] = jnp.zeros_like(acc_s)
        qt = _dot(wq_ref[...], xt)
        for hp in range(2 * PEER_HEADS):
            sc_s[hp] = _dot(keys_ref[hp], qt[hp * N_KEYS:(hp + 1) * N_KEYS].astype(BF16))

        def select(hc, exact):
            h = hc // n_chunks
            c0 = pl.multiple_of((hc % n_chunks) * LANE_CHUNK, LANE_CHUNK)
            lanes = pl.ds(c0, LANE_CHUNK)
            row = lax.broadcasted_iota(jnp.int32, (N_KEYS, LANE_CHUNK), 0).astype(F32)
            row8 = lax.broadcasted_iota(jnp.int32, (8, LANE_CHUNK), 0).astype(F32)
            s1 = sc_s[2 * h, :, lanes]
            s2 = sc_s[2 * h + 1, :, lanes]
            rank1, top1, cnt1 = _topk_rows(s1, row, exact)
            rank2, top2, cnt2 = _topk_rows(s2, row, exact)
            n_sel, z, cnt3 = _select_pairs(top1, top2, row8, exact)
            n_dense = jnp.zeros((N_KEYS, LANE_CHUNK), F32)
            for a in range(PEER_TOPK):
                n_dense = jnp.where(rank1 == float(a), n_sel[a], n_dense)
            n_s[h, :, lanes] = n_dense
            e1_s[h, :, lanes] = jnp.exp(s1 - top1[0])
            f = jnp.exp(s2 - top2[0]) * (0.5 / z)
            for rt in range(N_KEYS // SUB_ROWS):
                keys = slice(rt * SUB_ROWS, (rt + 1) * SUB_ROWS)
                rf_s[hc % n_chunks, rt, 2 * h] = rank2[keys]
                rf_s[hc % n_chunks, rt, 2 * h + 1] = f[keys]
            return jnp.max(jnp.maximum(jnp.maximum(cnt1, cnt2), cnt3))

        def per_head_pair(i, _):
            units = [RETRIEVE_UNROLL * i + k for k in range(RETRIEVE_UNROLL)]
            most = [select(hc, exact=False) for hc in units]
            for hc, m in zip(units, most):
                @pl.when(m > PEER_TOPK + 0.5)
                def _ties():
                    select(hc, exact=True)
            return 0

        lax.fori_loop(0, PEER_HEADS * n_chunks // RETRIEVE_UNROLL, per_head_pair, 0)

    ht_s[...] = _dot(u_ref[...], xt_s[...])

    def per_chunk(c, _):
        lanes = pl.ds(pl.multiple_of(c * LANE_CHUNK, LANE_CHUNK), LANE_CHUNK)
        grp = pl.ds(pl.multiple_of(e * rows_per_blk, rows_per_blk), rows_per_blk)
        n_grp = [n_s[h, grp, lanes] for h in range(PEER_HEADS)]
        e_grp = [e1_s[h, grp, lanes] for h in range(PEER_HEADS)]
        tile = (SUB_ROWS, LANE_CHUNK)
        for il in range(rows_per_blk):
            n8 = [jnp.broadcast_to(n_grp[h][il:il + 1], tile) for h in range(PEER_HEADS)]
            e8 = [jnp.broadcast_to(e_grp[h][il:il + 1], tile) for h in range(PEER_HEADS)]
            for rp in range(N_KEYS // PACK_ROWS):
                halves = []
                for rt in (2 * rp, 2 * rp + 1):
                    w = None
                    for h in range(PEER_HEADS):
                        t = jnp.where(rf_s[c, rt, 2 * h] < n8[h], rf_s[c, rt, 2 * h + 1], 0.0) * e8[h]
                        w = t if w is None else w + t
                    rows = pl.ds(il * N_KEYS + rt * SUB_ROWS, SUB_ROWS)
                    halves.append(_gelu2(ht_s[rows, lanes]) * w)
                rows = pl.ds(il * N_KEYS + rp * PACK_ROWS, PACK_ROWS)
                at_s[rows, lanes] = jnp.concatenate(halves, axis=0).astype(BF16)
        return 0

    lax.fori_loop(0, n_chunks, per_chunk, 0)
    acc_s[...] += _dot(vt_ref[...], at_s[...])

    @pl.when(e == pl.num_programs(1) - 1)
    def _emit():
        out_ref[...] = acc_s[...].T


def _peer(h_flat, wq_t, keys, u, v_t, tt, eb):
    t, d = h_flat.shape
    n_exp = u.shape[0]
    assert t % tt == 0 and eb == 8 * N_KEYS and n_exp % eb == 0
    scr = pltpu.VMEM
    return pl.pallas_call(
        functools.partial(_peer_kernel, tt=tt, eb=eb),
        grid=(t // tt, n_exp // eb),
        in_specs=[pl.BlockSpec((tt, d), lambda i, e: (i, 0)),
                  pl.BlockSpec(wq_t.shape, lambda i, e: (0, 0)),
                  pl.BlockSpec(keys.shape, lambda i, e: (0, 0, 0)),
                  pl.BlockSpec((eb, d), lambda i, e: (e, 0)),
                  pl.BlockSpec((d, eb), lambda i, e: (0, e))],
        out_specs=pl.BlockSpec((tt, d), lambda i, e: (i, 0)),
        out_shape=jax.ShapeDtypeStruct((t, d), F32),
        scratch_shapes=[scr((d, tt), BF16), scr((d, tt), F32), scr((eb, tt), F32), scr((eb, tt), BF16),
                        scr((2 * PEER_HEADS, N_KEYS, tt), F32),
                        scr((tt // LANE_CHUNK, N_KEYS // SUB_ROWS, 2 * PEER_HEADS, SUB_ROWS, LANE_CHUNK), F32),
                        scr((PEER_HEADS, N_KEYS, tt), F32), scr((PEER_HEADS, N_KEYS, tt), F32)],
        compiler_params=_cparams(("parallel", "arbitrary")),
        name="peer_dense",
    )(h_flat, wq_t, keys, u, v_t)


def _residual_kernel(x_ref, gate_ref, o_ref, *rest, final):
    x = x_ref[0] + gate_ref[0] * o_ref[0]
    if final:
        g_ref, out_ref = rest
        out_ref[0] = (x * lax.rsqrt(jnp.mean(x * x, axis=-1, keepdims=True) + EPS)) * g_ref[...]
    else:
        (out_ref,) = rest
        out_ref[0] = x


def _residual(x, gate, o, tt, g_final=None):
    n_seq, s, d = x.shape
    final = g_final is not None
    blk = pl.BlockSpec((1, tt, d), lambda b, i: (b, i, 0))
    in_specs = [blk, pl.BlockSpec((1, 1, d), lambda b, i: (b, 0, 0)), blk]
    args = [x, gate.reshape(n_seq, 1, d), o]
    if final:
        in_specs.append(pl.BlockSpec((1, d), lambda b, i: (0, 0)))
        args.append(g_final.reshape(1, d))
    return pl.pallas_call(
        functools.partial(_residual_kernel, final=final),
        grid=(n_seq, s // tt),
        in_specs=in_specs,
        out_specs=blk,
        out_shape=jax.ShapeDtypeStruct(x.shape, F32),
        compiler_params=_cparams(("parallel", "parallel")),
        name="residual",
    )(*args)


def _row_tile(s):
    return 512 if s % 512 == 0 else s


def _peer_tile(t):
    return 512 if t % 512 == 0 else 256


def _pad_lanes(x, n):
    return jnp.pad(x, [(0, 0)] * (x.ndim - 1) + [(0, n - x.shape[-1])])


def _prepare_weights(w_qkv_a, w_o_a, w_kv, w_f, b_f, w_q_b, w_o_b, w_pq, sub_keys, peer_u, peer_v):
    depth = w_pq.shape[0]

    def o_heads(w):
        w = w.reshape(N_HEADS, HEAD_DIM, D_MODEL)
        return jnp.pad(w, ((0, 0), (0, HEAD_LANES - HEAD_DIM), (0, 0))).astype(BF16)

    return dict(
        w_qkv=w_qkv_a[0].astype(BF16),
        w_o_a=o_heads(w_o_a[0]),
        w_kv=w_kv.astype(BF16),
        w_f=_pad_lanes(w_f, HEAD_LANES).astype(BF16),
        b_f=_pad_lanes(b_f, HEAD_LANES),
        w_q_b=w_q_b[0].astype(BF16),
        w_o_b=o_heads(w_o_b[0]),
        w_pq_t=[w_pq[l].T.astype(BF16) for l in range(depth)],
        keys=[jnp.swapaxes(sub_keys[l], 0, 1).reshape(2 * PEER_HEADS, N_KEYS, -1).astype(BF16)
              for l in range(depth)],
        u=[peer_u[l].astype(BF16) for l in range(depth)],
        v_t=[peer_v[l].T.astype(BF16) for l in range(depth)],
    )


def _peer_layer(x, h, gate, wts, l, g_final=None):
    n_seq, s, d = x.shape
    o = _peer(h.reshape(n_seq * s, d), wts["w_pq_t"][l], wts["keys"][l], wts["u"][l], wts["v_t"][l],
              tt=_peer_tile(n_seq * s), eb=8 * N_KEYS)
    return _residual(x, gate, o.reshape(n_seq, s, d), _row_tile(s), g_final=g_final)


def _with_cache(cache_heads, new_heads):
    return new_heads if cache_heads is None else jnp.concatenate([cache_heads, new_heads], axis=2)


def _trunk(x, mods0, mods1, modkv, caches, wts, g_norm, g_kv, g_final, rel_bias):
    n_seq, s, d = x.shape
    tt = _row_tile(s)
    sh_m, sc_m, gt_m, sh_c, sc_c, gt_c = jnp.split(mods0, 6, axis=-1)

    q_scale = HEAD_DIM ** -0.5
    heads = lambda scale, mode: (scale, mode)

    qh, kh, vh, k, v = _proj_heads(
        x, [(g_norm[0, 0], sh_m, sc_m, wts["w_qkv"])],
        [(0, 0, heads(q_scale, "plain")), (0, d, heads(1.0, "plain")), (0, 2 * d, heads(1.0, "ones")),
         (0, d, "nat"), (0, 2 * d, "nat")], tt)
    if caches is None:
        bias = _relbias_table(rel_bias, BAND, 2 * BAND, split=768, band_mask=True)
        oh = _band_attend_prompt(qh, kh, vh, bias)
        keep = min(BAND, s)
        new_ak, new_av = k[:, s - keep:], v[:, s - keep:]
    else:
        cache_a_k, cache_a_v = caches[0], caches[1]
        win = cache_a_k.shape[2]
        kh = _with_cache(_to_heads(cache_a_k[0].reshape(n_seq, win, d), _row_tile(win)), kh)
        vh = _with_cache(_to_heads(cache_a_v[0].reshape(n_seq, win, d), _row_tile(win), mode="ones"), vh)
        lanes = -(-(win + s) // 128) * 128
        bias = _relbias_table(rel_bias, s, win + s, split=(win + s + lanes - s) // 2, band_mask=False)
        oh = _band_attend_step(qh, kh, vh, bias)
        new_ak, new_av = k, v
    x, h = _oproj_residual(oh, wts["w_o_a"], x, gt_m, g_norm[0, 1], sh_c, sc_c, tt)
    x = _peer_layer(x, h, gt_c, wts, 0)

    sh_m, sc_m, gt_m, sh_c, sc_c, gt_c = jnp.split(mods1, 6, axis=-1)
    sh_kv, sc_kv = jnp.split(modkv, 2, axis=-1)
    (lf,) = _norm_proj(x, g_kv, sh_kv, sc_kv, wts["w_f"], (HEAD_LANES,), tt, bias_last=wts["b_f"])
    if caches is None:
        past = 0
        lf_all = lf
        tq = tk = 512 if s % 512 == 0 else s
        bs = 256 if s % 256 == 0 else s
    else:
        cache_b_k, cache_b_v, cache_b_logf = caches[2], caches[3], caches[4]
        past = cache_b_k.shape[1]
        lf_all = jnp.concatenate([_pad_lanes(cache_b_logf.astype(F32), HEAD_LANES), lf], axis=1)
        tq, tk = s, 512
        bs = s
    cum = _cumsum_seq(lf_all, bs)
    qh, kh, vh, bk, bv = _proj_heads(
        x, [(g_kv, sh_kv, sc_kv, wts["w_kv"]), (g_norm[1, 0], sh_m, sc_m, wts["w_q_b"])],
        [(1, 0, heads(q_scale, "q_aug")), (0, 0, heads(1.0, "k_aug")), (0, d, heads(1.0, "ones")),
         (0, 0, "nat"), (0, d, "nat")], tt, cum=cum[:, past:])
    if caches is not None:
        kh = _with_cache(_to_heads(cache_b_k.reshape(n_seq, past, d), _row_tile(past),
                                   mode="k_aug", cum=cum[:, :past]), kh)
        vh = _with_cache(_to_heads(cache_b_v.reshape(n_seq, past, d), _row_tile(past), mode="ones"), vh)
    oh = _fox_attend(qh, kh, vh, tq, tk, past)
    x, h = _oproj_residual(oh, wts["w_o_b"], x, gt_m, g_norm[1, 1], sh_c, sc_c, tt)
    y = _peer_layer(x, h, gt_c, wts, 1, g_final=g_final)

    hshape = (n_seq, -1, N_HEADS, HEAD_DIM)
    return (y, new_ak.reshape(hshape)[None], new_av.reshape(hshape)[None],
            bk.reshape(hshape), bv.reshape(hshape), lf[:, :, :N_HEADS])


def kernel(x_prompt, x_sample, c_prompt, c_sample, cache_a_k, cache_a_v, cache_b_k, cache_b_v, cache_b_logf, w_ada, b_ada, g_norm, w_qkv_a, rel_bias_a, w_o_a, w_ada_kv, b_ada_kv, g_kv, w_kv, w_f, b_f, w_q_b, w_o_b, w_pq, sub_keys, peer_u, peer_v, g_final):
    n_p, n_s = c_prompt.shape[0], c_sample.shape[0]
    n_c = -(-(n_p + n_s) // 16) * 16
    c_all = jnp.pad(jnp.concatenate([c_prompt, c_sample], axis=0), ((0, n_c - n_p - n_s), (0, 0)))
    mods0 = _cond_matmul(c_all, w_ada[0], b_ada[0])
    mods1 = _cond_matmul(c_all, w_ada[1], b_ada[1])
    modkv = _cond_matmul(c_all, w_ada_kv, b_ada_kv)
    wts = _prepare_weights(w_qkv_a, w_o_a, w_kv, w_f, b_f, w_q_b, w_o_b, w_pq, sub_keys, peer_u, peer_v)

    out_p = _trunk(x_prompt, mods0[:n_p], mods1[:n_p], modkv[:n_p], None,
                   wts, g_norm, g_kv, g_final, rel_bias_a[0])
    out_s = _trunk(x_sample, mods0[n_p:n_p + n_s], mods1[n_p:n_p + n_s], modkv[n_p:n_p + n_s],
                   (cache_a_k, cache_a_v, cache_b_k, cache_b_v, cache_b_logf),
                   wts, g_norm, g_kv, g_final, rel_bias_a[0])
    return (out_p[0], out_s[0]) + out_p[1:] + out_s[1:]
```
